```python
import jax, jax.numpy as jnp
from jax import lax
import numpy as np

D_MODEL = 1024
BATCH = 1
SEQ = 16384
DEPTH = 1
DEC_BATCH = 128
DEC_SEQ = 1
PAST_LEN = 16384
PAGE_SIZE = 128

MLA_HEADS = 8
MLA_Q_LORA = 384
MLA_KV_LORA = 256
MLA_NOPE = 64
MLA_ROPE = 32
MLA_V = 64
ROPE_THETA = 10000.0
MLA_Q_BLOCK = 128
MOBA_HEADS = 8
MOBA_HEAD_DIM = 64
MOBA_BLOCK = 256
MOBA_TOPK = 3
MOBA_Q_CHUNK = 64
N_EXPERTS = 32
TOP_K = 4
D_FF = 1024
SWIGLU_LIMIT = 7.0
SWIGLU_ALPHA = 1.702
MOE_ROW_BLOCK = 128

NORM_EPS = 1e-6
NEG = -1e30
MLA_QK = MLA_NOPE + MLA_ROPE
MLA_WIDTH = MLA_HEADS * MLA_V
MOBA_WIDTH = MOBA_HEADS * MOBA_HEAD_DIM
IN_SPLIT_SIZES = (MLA_Q_LORA, MLA_KV_LORA, MLA_ROPE, MOBA_WIDTH, MOBA_WIDTH, MOBA_WIDTH, D_MODEL, D_MODEL)
D_IN = sum(IN_SPLIT_SIZES)
IN_SPLIT_POINTS = tuple(int(v) for v in np.cumsum(IN_SPLIT_SIZES)[:-1])

kernel_name = "hybrid_mla_moba_moe_decode_step"


def rms_norm(x, g):
    xf = x.astype(jnp.float32)
    y = xf * lax.rsqrt(jnp.mean(xf * xf, axis=-1, keepdims=True) + NORM_EPS)
    return (y * g.astype(jnp.float32)).astype(x.dtype)


def apply_rope(x, pos):
    half = x.shape[-1] // 2
    inv_freq = ROPE_THETA ** (-jnp.arange(half, dtype=jnp.float32) / half)
    ang = pos.astype(jnp.float32)[:, None] * inv_freq[None, :]
    shape = (1, pos.shape[0]) + (1,) * (x.ndim - 3) + (half,)
    cos, sin = jnp.cos(ang).reshape(shape), jnp.sin(ang).reshape(shape)
    xf = x.astype(jnp.float32)
    x1, x2 = xf[..., :half], xf[..., half:]
    return jnp.concatenate([x1 * cos - x2 * sin, x2 * cos + x1 * sin], axis=-1).astype(x.dtype)


def alibi_slopes():
    return 2.0 ** (-8.0 * jnp.arange(1, MOBA_HEADS + 1, dtype=jnp.float32) / MOBA_HEADS)


def mixer_inputs(h, pos, w_in, g_cq, w_uq, g_ckv):
    B, S, _ = h.shape
    cq, ckv, kr, qm, km, vm, ga, gb = jnp.split(h @ w_in, IN_SPLIT_POINTS, axis=-1)
    q = (rms_norm(cq, g_cq) @ w_uq).reshape(B, S, MLA_HEADS, MLA_QK)
    q_nope = q[..., :MLA_NOPE]
    q_rope = apply_rope(q[..., MLA_NOPE:], pos)
    ckv = rms_norm(ckv, g_ckv)
    kr = apply_rope(kr, pos)
    shp = (B, S, MOBA_HEADS, MOBA_HEAD_DIM)
    return q_nope, q_rope, ckv, kr, qm.reshape(shp), km.reshape(shp), vm.reshape(shp), ga, gb


def mla_prompt(q_nope, q_rope, ckv, kr, w_uk, w_uv):
    B, S, H, _ = q_nope.shape
    k_nope = jnp.einsum('bsc,chn->bshn', ckv, w_uk)
    v = jnp.einsum('bsc,chv->bshv', ckv, w_uv)
    scale = MLA_QK ** -0.5
    nb = S // MLA_Q_BLOCK
    qn = q_nope.reshape(B, nb, MLA_Q_BLOCK, H, MLA_NOPE).swapaxes(0, 1)
    qr = q_rope.reshape(B, nb, MLA_Q_BLOCK, H, MLA_ROPE).swapaxes(0, 1)
    key_pos = jnp.arange(S)

    def block(args):
        qn_b, qr_b, i = args
        s = jnp.einsum('bqhn,bkhn->bhqk', qn_b, k_nope) + jnp.einsum('bqhr,bkr->bhqk', qr_b, kr)
        s = s.astype(jnp.float32) * scale
        q_pos = i * MLA_Q_BLOCK + jnp.arange(MLA_Q_BLOCK)
        s = jnp.where(key_pos[None, :] <= q_pos[:, None], s, NEG)
        p = jax.nn.softmax(s, axis=-1).astype(v.dtype)
        return jnp.einsum('bhqk,bkhv->bqhv', p, v)

    o = lax.map(block, (qn, qr, jnp.arange(nb)))
    return o.swapaxes(0, 1).reshape(B, S, H * MLA_V)


def mla_sample(q_nope, q_rope, ckv_new, kr_new, ckv_pool, kr_pool, page_table, layer, w_uk, w_uv):
    B, Q, H, _ = q_nope.shape
    f32 = jnp.float32
    scale = MLA_QK ** -0.5
    q_lat = jnp.einsum('bqhn,chn->bqhc', q_nope, w_uk)

    def page_step(carry, phys):
        m, l, acc = carry
        ckv = ckv_pool[layer, phys]
        kr = kr_pool[layer, phys]
        s = (jnp.einsum('bqhc,bkc->bqhk', q_lat, ckv) + jnp.einsum('bqhr,bkr->bqhk', q_rope, kr)).astype(f32) * scale
        m_new = jnp.maximum(m, s.max(-1))
        corr = jnp.exp(m - m_new)
        p = jnp.exp(s - m_new[..., None])
        l = l * corr + p.sum(-1)
        acc = acc * corr[..., None] + jnp.einsum('bqhk,bkc->bqhc', p, ckv.astype(f32))
        return (m_new, l, acc), None

    init = (jnp.full((B, Q, H), NEG, f32), jnp.zeros((B, Q, H), f32), jnp.zeros((B, Q, H, MLA_KV_LORA), f32))
    (m, l, acc), _ = lax.scan(page_step, init, page_table.T)
    s = (jnp.einsum('bqhc,bkc->bqhk', q_lat, ckv_new) + jnp.einsum('bqhr,bkr->bqhk', q_rope, kr_new)).astype(f32) * scale
    causal = (jnp.arange(Q)[None, :] <= jnp.arange(Q)[:, None])[None, :, None, :]
    s = jnp.where(causal, s, NEG)
    m_new = jnp.maximum(m, s.max(-1))
    corr = jnp.exp(m - m_new)
    p = jnp.exp(s - m_new[..., None])
    l = l * corr + p.sum(-1)
    acc = acc * corr[..., None] + jnp.einsum('bqhk,bkc->bqhc', p, ckv_new.astype(f32))
    o = jnp.einsum('bqhc,chv->bqhv', (acc / l[..., None]).astype(q_nope.dtype), w_uv)
    return o.reshape(B, Q, H * MLA_V)


def moba_attend(q, q_pos, block_means, gather_kv):
    B, Q, H, _ = q.shape
    n_blocks = block_means.shape[1]
    own = q_pos // MOBA_BLOCK
    gate = jnp.einsum('bqhd,bnhd->bqhn', q, block_means).astype(jnp.float32)
    eligible = jnp.arange(n_blocks)[None, :] < own[:, None]
    gate = jnp.where(eligible[None, :, None, :], gate, NEG)
    n_top = min(MOBA_TOPK, n_blocks)
    _, top = lax.top_k(gate, n_top)
    own_b = jnp.broadcast_to(own[None, :, None, None], (B, Q, H, 1)).astype(top.dtype)
    sel = jnp.concatenate([top, own_b], axis=-1)
    top_ok = jnp.broadcast_to((jnp.arange(n_top)[None, :] < own[:, None])[None, :, None, :], (B, Q, H, n_top))
    valid = jnp.concatenate([top_ok, jnp.ones((B, Q, H, 1), bool)], axis=-1)
    k_g, v_g = gather_kv(sel)
    key_pos = sel[..., None] * MOBA_BLOCK + jnp.arange(MOBA_BLOCK)
    dist = q_pos[None, :, None, None, None] - key_pos
    mask = valid[..., None] & (dist >= 0)
    s = jnp.einsum('bqhd,bqhnkd->bqhnk', q, k_g).astype(jnp.float32) * MOBA_HEAD_DIM ** -0.5
    s = s - alibi_slopes()[None, None, :, None, None] * dist.astype(jnp.float32)
    s = jnp.where(mask, s, NEG).reshape(B, Q, H, -1)
    p = jax.nn.softmax(s, axis=-1).reshape(k_g.shape[:-1]).astype(v_g.dtype)
    return jnp.einsum('bqhnk,bqhnkd->bqhd', p, v_g)


def moba_prompt(q, k, v):
    B, S, H, Dh = q.shape
    nb = -(-S // MOBA_BLOCK)
    pad = ((0, 0), (0, nb * MOBA_BLOCK - S), (0, 0), (0, 0))
    kb = jnp.pad(k, pad).reshape(B, nb, MOBA_BLOCK, H, Dh)
    vb = jnp.pad(v, pad).reshape(B, nb, MOBA_BLOCK, H, Dh)
    means = kb.mean(axis=2)
    kbt, vbt = kb.transpose(0, 3, 1, 2, 4), vb.transpose(0, 3, 1, 2, 4)
    bi = jnp.arange(B)[:, None, None, None]
    hi = jnp.arange(H)[None, None, :, None]

    def gather_kv(sel):
        return kbt[bi, hi, sel], vbt[bi, hi, sel]

    nq = S // MOBA_Q_CHUNK
    qc = q.reshape(B, nq, MOBA_Q_CHUNK, H, Dh).swapaxes(0, 1)

    def chunk(args):
        q_blk, i = args
        return moba_attend(q_blk, i * MOBA_Q_CHUNK + jnp.arange(MOBA_Q_CHUNK), means, gather_kv)

    o = lax.map(chunk, (qc, jnp.arange(nq)))
    return o.swapaxes(0, 1).reshape(B, S, H * Dh)


def moba_sample(q, k_new, v_new, k_pool, v_pool, page_table, layer):
    B, Q, H, Dh = q.shape
    ppb = MOBA_BLOCK // PAGE_SIZE
    past = page_table.shape[1] * PAGE_SIZE
    nb_past = past // MOBA_BLOCK
    q_pos = past + jnp.arange(Q)
    tail_pages = page_table[:, nb_past * ppb:]
    n_tail_rows = tail_pages.shape[1] * PAGE_SIZE
    k_tail = jnp.concatenate([k_pool[layer, tail_pages].reshape(B, n_tail_rows, H, Dh), k_new], axis=1)
    v_tail = jnp.concatenate([v_pool[layer, tail_pages].reshape(B, n_tail_rows, H, Dh), v_new], axis=1)
    nb_tail = -(-k_tail.shape[1] // MOBA_BLOCK)
    pad = ((0, 0), (0, nb_tail * MOBA_BLOCK - k_tail.shape[1]), (0, 0), (0, 0))
    kt = jnp.pad(k_tail, pad).reshape(B, nb_tail, MOBA_BLOCK, H, Dh)
    vt = jnp.pad(v_tail, pad).reshape(B, nb_tail, MOBA_BLOCK, H, Dh)
    means = kt.mean(axis=2)
    ktt, vtt = kt.transpose(0, 3, 1, 2, 4), vt.transpose(0, 3, 1, 2, 4)
    bi = jnp.arange(B)[:, None, None, None]
    hi = jnp.arange(H)[None, None, :, None]
    if nb_past > 0:
        blk_pages = page_table[:, :nb_past * ppb].reshape(B, nb_past, ppb).swapaxes(0, 1)
        past_means = lax.map(lambda pg: k_pool[layer, pg].reshape(B, MOBA_BLOCK, H, Dh).mean(axis=1), blk_pages)
        means = jnp.concatenate([past_means.swapaxes(0, 1), means], axis=1)

    def gather_kv(sel):
        ts = jnp.clip(sel - nb_past, 0, nb_tail - 1)
        k_g, v_g = ktt[bi, hi, ts], vtt[bi, hi, ts]
        if nb_past == 0:
            return k_g, v_g
        pb = jnp.clip(sel, 0, nb_past - 1)
        phys = page_table[bi[..., None], pb[..., None] * ppb + jnp.arange(ppb)]
        hi5 = hi[..., None]
        shp = sel.shape + (MOBA_BLOCK, Dh)
        k_p = k_pool[layer, phys, :, hi5].reshape(shp)
        v_p = v_pool[layer, phys, :, hi5].reshape(shp)
        is_past = (sel < nb_past)[..., None, None]
        return jnp.where(is_past, k_p, k_g), jnp.where(is_past, v_p, v_g)

    return moba_attend(q, q_pos, means, gather_kv).reshape(B, Q, H * Dh)


def moe_ffn(h, w_router, b_router, w_gu, b_gu, w_down, b_down):
    lead = h.shape[:-1]
    x = h.reshape(-1, h.shape[-1])
    T = x.shape[0]
    A = T * TOP_K
    RB = MOE_ROW_BLOCK
    logits = (x @ w_router + b_router).astype(jnp.float32)
    top_logit, top_e = lax.top_k(logits, TOP_K)
    weights = jax.nn.softmax(top_logit, axis=-1)
    flat_e = top_e.reshape(A).astype(jnp.int32)
    order = jnp.argsort(flat_e * A + jnp.arange(A, dtype=jnp.int32))
    sorted_e = flat_e[order]
    counts = jnp.bincount(flat_e, length=N_EXPERTS).astype(jnp.int32)
    padded = (counts + RB - 1) // RB * RB
    pad_end = jnp.cumsum(padded)
    pad_start = pad_end - padded
    start = jnp.cumsum(counts) - counts
    row_sorted = (pad_start[sorted_e] + jnp.arange(A, dtype=jnp.int32) - start[sorted_e]).astype(jnp.int32)
    row = jnp.zeros((A,), jnp.int32).at[order].set(row_sorted)
    n_blk = -(-A // RB) + N_EXPERTS
    token_of_row = jnp.full((n_blk * RB,), T, jnp.int32).at[row].set(jnp.arange(A, dtype=jnp.int32) // TOP_K)
    x_rows = jnp.concatenate([x, jnp.zeros((1, x.shape[1]), x.dtype)], axis=0)[token_of_row]
    x_rows = x_rows.reshape(n_blk, RB, x.shape[1])
    blk_expert = jnp.minimum(jnp.searchsorted(pad_end, jnp.arange(n_blk, dtype=jnp.int32) * RB, side='right'), N_EXPERTS - 1)

    def expert_block(args):
        xb, e = args
        gu = xb @ w_gu[e] + b_gu[e]
        g = jnp.minimum(gu[:, ::2], SWIGLU_LIMIT)
        u = jnp.clip(gu[:, 1::2], -SWIGLU_LIMIT, SWIGLU_LIMIT)
        a = (u + 1.0) * (g * jax.nn.sigmoid(SWIGLU_ALPHA * g))
        return a @ w_down[e] + b_down[e]

    y_rows = lax.map(expert_block, (x_rows, blk_expert)).reshape(n_blk * RB, -1)
    y = (y_rows[row].reshape(T, TOP_K, -1) * weights[..., None].astype(y_rows.dtype)).sum(axis=1)
    return y.reshape(lead + (y.shape[-1],))


def decoder_layer(x, c, pos, attend, lw):
    (w_ada, b_ada, g_pre_mix, g_post_mix, g_pre_ffn, g_post_ffn, w_in, g_cq, w_uq, g_ckv, w_uk, w_uv,
     w_br_mla, w_br_moba, w_out, w_router, b_router, w_gu, b_gu, w_down, b_down) = lw
    sh1, sc1, gt1, sh2, sc2, gt2 = jnp.split((jax.nn.silu(c) @ w_ada + b_ada)[:, None, :], 6, axis=-1)
    h = rms_norm(x, g_pre_mix) * (1.0 + sc1) + sh1
    q_nope, q_rope, ckv, kr, qm, km, vm, ga, gb = mixer_inputs(h, pos, w_in, g_cq, w_uq, g_ckv)
    o_mla, o_moba = attend(q_nope, q_rope, ckv, kr, qm, km, vm, w_uk, w_uv)
    merged = jax.nn.sigmoid(ga) * (o_mla @ w_br_mla) + jax.nn.sigmoid(gb) * (o_moba @ w_br_moba)
    x = x + gt1 * rms_norm(merged @ w_out, g_post_mix)
    h = rms_norm(x, g_pre_ffn) * (1.0 + sc2) + sh2
    x = x + gt2 * rms_norm(moe_ffn(h, w_router, b_router, w_gu, b_gu, w_down, b_down), g_post_ffn)
    return x, (ckv, kr, km, vm)


def setup_inputs(seed: int = 0) -> dict:
    key = jax.random.key(seed)
    ks = iter(jax.random.split(key, 40))
    f32 = jnp.float32
    n_pages = PAST_LEN // PAGE_SIZE
    n_used = DEC_BATCH * n_pages
    n_pool = n_used + n_used // 4

    def nrm(shape, scale):
        return jax.random.normal(next(ks), shape, f32) * scale

    def gain(shape):
        return 1.0 + nrm(shape, 0.1)

    D = D_MODEL
    x_prompt = jax.random.normal(next(ks), (BATCH, SEQ, D), f32)
    x_sample = jax.random.normal(next(ks), (DEC_BATCH, DEC_SEQ, D), f32)
    cache_mla_ckv = jax.random.normal(next(ks), (DEPTH, n_pool, PAGE_SIZE, MLA_KV_LORA), f32)
    cache_mla_krope = jax.random.normal(next(ks), (DEPTH, n_pool, PAGE_SIZE, MLA_ROPE), f32)
    cache_moba_k = jax.random.normal(next(ks), (DEPTH, n_pool, PAGE_SIZE, MOBA_HEADS, MOBA_HEAD_DIM), f32)
    cache_moba_v = jax.random.normal(next(ks), (DEPTH, n_pool, PAGE_SIZE, MOBA_HEADS, MOBA_HEAD_DIM), f32)
    perm = jax.random.permutation(next(ks), n_pool)
    page_table = perm[:n_used].reshape(DEC_BATCH, n_pages).astype(jnp.int32)
    c_prompt = jax.random.normal(next(ks), (BATCH, D), f32)
    c_sample = jax.random.normal(next(ks), (DEC_BATCH, D), f32)
    return {
        'x_prompt': x_prompt, 'x_sample': x_sample,
        'cache_mla_ckv': cache_mla_ckv, 'cache_mla_krope': cache_mla_krope,
        'cache_moba_k': cache_moba_k, 'cache_moba_v': cache_moba_v,
        'page_table': page_table, 'c_prompt': c_prompt, 'c_sample': c_sample,
        'w_ada': nrm((DEPTH, D, 6 * D), D ** -0.5), 'b_ada': nrm((DEPTH, 6 * D), 0.02),
        'g_pre_mix': gain((DEPTH, D)), 'g_post_mix': gain((DEPTH, D)),
        'g_pre_ffn': gain((DEPTH, D)), 'g_post_ffn': gain((DEPTH, D)),
        'w_in': nrm((DEPTH, D, D_IN), D ** -0.5),
        'g_cq': gain((DEPTH, MLA_Q_LORA)),
        'w_uq': nrm((DEPTH, MLA_Q_LORA, MLA_HEADS * MLA_QK), MLA_Q_LORA ** -0.5),
        'g_ckv': gain((DEPTH, MLA_KV_LORA)),
        'w_uk': nrm((DEPTH, MLA_KV_LORA, MLA_HEADS, MLA_NOPE), MLA_KV_LORA ** -0.5),
        'w_uv': nrm((DEPTH, MLA_KV_LORA, MLA_HEADS, MLA_V), MLA_KV_LORA ** -0.5),
        'w_br_mla': nrm((DEPTH, MLA_WIDTH, D), MLA_WIDTH ** -0.5),
        'w_br_moba': nrm((DEPTH, MOBA_WIDTH, D), MOBA_WIDTH ** -0.5),
        'w_out': nrm((DEPTH, D, D), D ** -0.5),
        'w_router': nrm((DEPTH, D, N_EXPERTS), D ** -0.5),
        'b_router': nrm((DEPTH, N_EXPERTS), 0.01),
        'w_gate_up': nrm((DEPTH, N_EXPERTS, D, 2 * D_FF), D ** -0.5),
        'b_gate_up': nrm((DEPTH, N_EXPERTS, 2 * D_FF), 0.01),
        'w_down': nrm((DEPTH, N_EXPERTS, D_FF, D), D_FF ** -0.5),
        'b_down': nrm((DEPTH, N_EXPERTS, D), 0.01),
    }


def reference(x_prompt, x_sample, cache_mla_ckv, cache_mla_krope, cache_moba_k, cache_moba_v, page_table,
              c_prompt, c_sample, w_ada, b_ada, g_pre_mix, g_post_mix, g_pre_ffn, g_post_ffn, w_in, g_cq, w_uq,
              g_ckv, w_uk, w_uv, w_br_mla, w_br_moba, w_out, w_router, b_router, w_gate_up, b_gate_up,
              w_down, b_down):
    past = page_table.shape[1] * PAGE_SIZE
    pos_prompt = jnp.arange(x_prompt.shape[1])
    pos_sample = past + jnp.arange(x_sample.shape[1])

    def attend_prompt(q_nope, q_rope, ckv, kr, qm, km, vm, w_uk_l, w_uv_l):
        return mla_prompt(q_nope, q_rope, ckv, kr, w_uk_l, w_uv_l), moba_prompt(qm, km, vm)

    yp, ys = x_prompt, x_sample
    st_p, st_s = [], []
    for l in range(DEPTH):
        lw = (w_ada[l], b_ada[l], g_pre_mix[l], g_post_mix[l], g_pre_ffn[l], g_post_ffn[l], w_in[l], g_cq[l],
              w_uq[l], g_ckv[l], w_uk[l], w_uv[l], w_br_mla[l], w_br_moba[l], w_out[l], w_router[l],
              b_router[l], w_gate_up[l], b_gate_up[l], w_down[l], b_down[l])

        def attend_sample(q_nope, q_rope, ckv, kr, qm, km, vm, w_uk_l, w_uv_l, layer=l):
            o_a = mla_sample(q_nope, q_rope, ckv, kr, cache_mla_ckv, cache_mla_krope, page_table, layer, w_uk_l, w_uv_l)
            o_b = moba_sample(qm, km, vm, cache_moba_k, cache_moba_v, page_table, layer)
            return o_a, o_b

        yp, sp = decoder_layer(yp, c_prompt, pos_prompt, attend_prompt, lw)
        ys, ss = decoder_layer(ys, c_sample, pos_sample, attend_sample, lw)
        st_p.append(sp)
        st_s.append(ss)

    new_ckv_prompt = jnp.stack([s[0] for s in st_p])
    new_krope_prompt = jnp.stack([s[1] for s in st_p])
    new_k_prompt = jnp.stack([s[2] for s in st_p])
    new_v_prompt = jnp.stack([s[3] for s in st_p])
    new_ckv_sample = jnp.stack([s[0] for s in st_s])
    new_krope_sample = jnp.stack([s[1] for s in st_s])
    new_k_sample = jnp.stack([s[2] for s in st_s])
    new_v_sample = jnp.stack([s[3] for s in st_s])
    return (yp, ys, new_ckv_prompt, new_krope_prompt, new_k_prompt, new_v_prompt,
            new_ckv_sample, new_krope_sample, new_k_sample, new_v_sample)
```

```python
import functools

import jax
import jax.numpy as jnp
from jax import lax
from jax.experimental import pallas as pl
from jax.experimental.pallas import tpu as pltpu

F32 = jnp.float32
BF16 = jnp.bfloat16
I32 = jnp.int32

NORM_EPS = 1e-6
NEG = -1e30
ROPE_THETA = 10000.0

MLA_HEADS = 8
MLA_NOPE = 64
MLA_ROPE = 32
MLA_V = 64
MLA_QK = MLA_NOPE + MLA_ROPE
MLA_Q_LORA = 384
MLA_KV_LORA = 256
HEAD_LANES = 128
MOBA_HEADS = 8
MOBA_HEAD_DIM = 64
MOBA_BLOCK = 256
MOBA_TOPK = 3
MOBA_WIDTH = MOBA_HEADS * MOBA_HEAD_DIM
PAGE_SIZE = 128
PAGES_PER_BLOCK = MOBA_BLOCK // PAGE_SIZE
N_EXPERTS = 32
TOP_K = 4
SWIGLU_LIMIT = 7.0
SWIGLU_ALPHA = 1.702

ROW_TILE = 256
MLA_TQ = 256
MLA_TK = 256
DECODE_PAGES = 8
MOE_ROWS = 256
COMBINE_ROWS = 128
VMEM_LIMIT = 48 * 1024 * 1024

_NT = (((1,), (1,)), ((), ()))


def _rms(x, g):
    return x * lax.rsqrt(jnp.mean(x * x, axis=-1, keepdims=True) + NORM_EPS) * g


def _dot(a, b):
    return jnp.dot(a, b, preferred_element_type=F32)


def _dot_nt(a, b):
    return lax.dot_general(a, b, _NT, preferred_element_type=F32)


def _params(*sem):
    return pltpu.CompilerParams(dimension_semantics=sem, vmem_limit_bytes=VMEM_LIMIT)


def _ada_kernel(c_ref, w_ref, b_ref, o_ref):
    c = c_ref[...]
    s = (c * jax.nn.sigmoid(c)).astype(BF16)
    o_ref[...] = _dot(s, w_ref[...].astype(BF16)) + b_ref[...]


def _ada(c_all, w_ada, b_ada):
    rows, d = c_all.shape
    n = w_ada.shape[1]
    tn = 1024
    return pl.pallas_call(
        _ada_kernel,
        out_shape=jax.ShapeDtypeStruct((rows, n), F32),
        grid=(n // tn,),
        in_specs=[pl.BlockSpec((rows, d), lambda j: (0, 0)),
                  pl.BlockSpec((d, tn), lambda j: (0, j)),
                  pl.BlockSpec((1, tn), lambda j: (0, j))],
        out_specs=pl.BlockSpec((rows, tn), lambda j: (0, j)),
        compiler_params=_params("arbitrary"),
        name="ada",
    )(c_all, w_ada, b_ada.reshape(1, n))


def _inproj_kernel(x_ref, sh_ref, sc_ref, gpre_ref, wa_ref, wkr_ref, wm_ref, wg_ref, gcq_ref, gckv_ref,
                   wqm_ref, wqs_ref, tc_ref, ts_ref, wuk_ref, ekr_ref, wuv_ref, c32_ref, s32_ref,
                   q_ref, k_ref, v_ref, ckv_ref, kr_ref, qm_ref, km_ref, vm_ref, kmb_ref, vmb_ref,
                   sg_ref, mean_ref):
    x = x_ref[...]
    h = (_rms(x, gpre_ref[...]) * (1.0 + sc_ref[...]) + sh_ref[...]).astype(BF16)

    ya = _dot(h, wa_ref[...])
    cq = _rms(ya[:, :MLA_Q_LORA], gcq_ref[...]).astype(BF16)
    ckv = _rms(ya[:, MLA_Q_LORA:], gckv_ref[...])
    ckv_ref[...] = ckv
    ckv_b = ckv.astype(BF16)

    qmain = _dot(cq, wqm_ref[...])
    qswap = _dot(cq, wqs_ref[...])
    tc = tc_ref[...]
    ts = ts_ref[...]
    for hd in range(MLA_HEADS):
        sl = slice(hd * HEAD_LANES, (hd + 1) * HEAD_LANES)
        q_ref[:, sl] = (qmain[:, sl] * tc + qswap[:, sl] * ts).astype(BF16)

    ykr = _dot(h, wkr_ref[...])
    kr = ykr[:, :MLA_ROPE] * c32_ref[...] + ykr[:, MLA_ROPE:] * s32_ref[...]
    kr_ref[...] = kr
    k_ref[...] = (_dot(ckv_b, wuk_ref[...]) + _dot(kr.astype(BF16), ekr_ref[...])).astype(BF16)
    v_ref[...] = _dot(ckv_b, wuv_ref[...]).astype(BF16)

    ym = _dot(h, wm_ref[...])
    w = MOBA_WIDTH
    qm_ref[...] = (ym[:, :w] * (MOBA_HEAD_DIM ** -0.5)).astype(BF16)
    km = ym[:, w:2 * w]
    vm = ym[:, 2 * w:]
    km_ref[...] = km
    vm_ref[...] = vm
    kmb_ref[...] = km.astype(BF16)
    vmb_ref[...] = vm.astype(BF16)
    nb = mean_ref.shape[1]
    rows = km.shape[0] // nb
    for b in range(nb):
        mean_ref[0, b:b + 1, :] = jnp.mean(km[b * rows:(b + 1) * rows], axis=0, keepdims=True)

    sg_ref[...] = jax.nn.sigmoid(_dot(h, wg_ref[...])).astype(BF16)


def _inproj(x, mod, gpre, wts, tabs):
    rows, d = x.shape
    tm = min(ROW_TILE, rows)
    mrows = mod.shape[0]
    mt = tm if mrows > 1 else 1
    mrow = (lambda i: i) if mrows > 1 else (lambda i: 0)
    nb = max(tm // MOBA_BLOCK, 1)
    nt = rows // tm

    def full(a):
        return pl.BlockSpec(a.shape, lambda i: (0,) * a.ndim)

    def rowblk(n):
        return pl.BlockSpec((tm, n), lambda i: (i, 0))

    tc, ts, c32, s32 = tabs
    in_specs = [rowblk(d),
                pl.BlockSpec((mt, d), lambda i: (mrow(i), 0)),
                pl.BlockSpec((mt, d), lambda i: (mrow(i), 1)),
                full(gpre), full(wts["w_a"]), full(wts["w_kr"]), full(wts["w_m"]), full(wts["w_g"]),
                full(wts["g_cq"]), full(wts["g_ckv"]), full(wts["wq_main"]), full(wts["wq_swap"]),
                rowblk(HEAD_LANES), rowblk(HEAD_LANES), full(wts["w_uk_pad"]), full(wts["e_kr"]),
                full(wts["w_uv2"]), rowblk(MLA_ROPE), rowblk(MLA_ROPE)]
    hp = MLA_HEADS * HEAD_LANES
    out_shape = [jax.ShapeDtypeStruct((rows, hp), BF16),
                 jax.ShapeDtypeStruct((rows, hp), BF16),
                 jax.ShapeDtypeStruct((rows, MLA_HEADS * MLA_V), BF16),
                 jax.ShapeDtypeStruct((rows, MLA_KV_LORA), F32),
                 jax.ShapeDtypeStruct((rows, MLA_ROPE), F32),
                 jax.ShapeDtypeStruct((rows, MOBA_WIDTH), BF16),
                 jax.ShapeDtypeStruct((rows, MOBA_WIDTH), F32),
                 jax.ShapeDtypeStruct((rows, MOBA_WIDTH), F32),
                 jax.ShapeDtypeStruct((rows, MOBA_WIDTH), BF16),
                 jax.ShapeDtypeStruct((rows, MOBA_WIDTH), BF16),
                 jax.ShapeDtypeStruct((rows, 2 * d), BF16),
                 jax.ShapeDtypeStruct((nt, nb, MOBA_WIDTH), F32)]
    out_specs = [rowblk(hp), rowblk(hp), rowblk(MLA_HEADS * MLA_V), rowblk(MLA_KV_LORA), rowblk(MLA_ROPE),
                 rowblk(MOBA_WIDTH), rowblk(MOBA_WIDTH), rowblk(MOBA_WIDTH), rowblk(MOBA_WIDTH),
                 rowblk(MOBA_WIDTH), rowblk(2 * d),
                 pl.BlockSpec((1, nb, MOBA_WIDTH), lambda i: (i, 0, 0))]
    return pl.pallas_call(
        _inproj_kernel,
        out_shape=out_shape,
        grid=(nt,),
        in_specs=in_specs,
        out_specs=out_specs,
        compiler_params=_params("arbitrary"),
        name="inproj",
    )(x, mod, mod, gpre, wts["w_a"], wts["w_kr"], wts["w_m"], wts["w_g"], wts["g_cq"], wts["g_ckv"],
      wts["wq_main"], wts["wq_swap"], tc, ts, wts["w_uk_pad"], wts["e_kr"], wts["w_uv2"], c32, s32)


def _mla_prompt_kernel(q_ref, k_ref, v_ref, o_ref, *, tq, tk):
    qi = pl.program_id(1)
    lane = lax.broadcasted_iota(I32, (tq, HEAD_LANES), 1)
    rel = (lax.broadcasted_iota(I32, (tq, tk), 1) - lax.broadcasted_iota(I32, (tq, tk), 0))
    outs = []
    for hh in range(2):
        hs = slice(hh * HEAD_LANES, (hh + 1) * HEAD_LANES)
        q = q_ref[:, hs]

        def step(kb, carry, masked):
            m, l, acc = carry
            k0 = pl.multiple_of(kb * tk, tk)
            k = k_ref[pl.ds(k0, tk), hs]
            v = v_ref[pl.ds(k0, tk), :]
            s = _dot_nt(q, k)
            if masked:
                s = jnp.where(rel <= qi * tq - kb * tk, s, NEG)
            m_new = jnp.maximum(m, jnp.max(s, axis=-1, keepdims=True))
            corr = jnp.exp(m - m_new)
            p = jnp.exp(s - m_new)
            l = l * corr + jnp.sum(p, axis=-1, keepdims=True)
            acc = acc * corr + _dot(p.astype(BF16), v)
            return m_new, l, acc

        carry = (jnp.full((tq, 1), NEG, F32), jnp.zeros((tq, 1), F32), jnp.zeros((tq, HEAD_LANES), F32))
        nfull = (qi * tq) // tk
        carry = lax.fori_loop(0, nfull, functools.partial(step, masked=False), carry)
        for dblk in range(tq // tk):
            carry = step(nfull + dblk, carry, True)
        _, l, acc = carry
        outs.append(acc / l)
    o_ref[...] = jnp.where(lane < MLA_V, outs[0], outs[1]).astype(BF16)


def _mla_prompt(q, k, v):
    s = q.shape[0]
    tq = min(MLA_TQ, s)
    tk = min(MLA_TK, tq)
    pairs = MLA_HEADS // 2
    return pl.pallas_call(
        functools.partial(_mla_prompt_kernel, tq=tq, tk=tk),
        out_shape=jax.ShapeDtypeStruct((s, MLA_HEADS * MLA_V), BF16),
        grid=(pairs, s // tq),
        in_specs=[pl.BlockSpec((tq, 2 * HEAD_LANES), lambda p, i: (i, p)),
                  pl.BlockSpec((s, 2 * HEAD_LANES), lambda p, i: (0, p)),
                  pl.BlockSpec((s, 2 * MLA_V), lambda p, i: (0, p))],
        out_specs=pl.BlockSpec((tq, 2 * MLA_V), lambda p, i: (i, p)),
        compiler_params=_params("arbitrary", "arbitrary"),
        name="mla_prompt",
    )(q, k, v)


def _top3_penalty(gate, valid_lane, own_lane, n_elig):
    work = jnp.where(valid_lane, gate, -jnp.inf)
    lanes = lax.broadcasted_iota(I32, gate.shape, 1)
    pen = jnp.full(gate.shape, NEG, F32)
    for j in range(MOBA_TOPK):
        mx = jnp.max(work, axis=-1, keepdims=True)
        first = jnp.min(jnp.where(work == mx, lanes, 2 * gate.shape[1]), axis=-1, keepdims=True)
        pick = lanes == first
        pen = jnp.where(pick, jnp.where(j < n_elig, 0.0, NEG), pen)
        work = jnp.where(pick, -jnp.inf, work)
    return jnp.where(own_lane, 0.0, pen)


def _moba_prompt_kernel(slope_ref, q_ref, k_ref, v_ref, mean_ref, o_ref):
    pr = pl.program_id(0)
    qi = pl.program_id(1)
    tq = q_ref.shape[0]
    bs = MOBA_BLOCK
    lane = lax.broadcasted_iota(I32, (tq, 2 * MOBA_HEAD_DIM), 1)
    blk = lane & (MOBA_HEAD_DIM - 1)
    klane = lax.broadcasted_iota(I32, (bs, 2 * MOBA_HEAD_DIM), 1)
    kblk = klane & (MOBA_HEAD_DIM - 1)
    col = lax.broadcasted_iota(I32, (1, bs), 1)
    rel = lax.broadcasted_iota(I32, (tq, bs), 1) - lax.broadcasted_iota(I32, (tq, bs), 0)
    q2 = q_ref[...]
    means = mean_ref[...].astype(BF16)
    means2 = jnp.concatenate([means, means], axis=0)
    outs = []
    for hh in range(2):
        mine = (lane < MOBA_HEAD_DIM) if hh == 0 else (lane >= MOBA_HEAD_DIM)
        kmine = (klane < MOBA_HEAD_DIM) if hh == 0 else (klane >= MOBA_HEAD_DIM)
        slope = slope_ref[2 * pr + hh]
        qh = jnp.where(mine, q2, jnp.zeros_like(q2))
        gate = _dot_nt(qh, means2)
        gate = jnp.where(blk < qi, gate, NEG)
        pen = _top3_penalty(gate, jnp.logical_not(mine), blk == qi, qi).astype(BF16)
        qa = jnp.where(mine, q2, pen)

        def step(n, carry, own):
            m, l, acc = carry
            k0 = pl.multiple_of(n * bs, bs)
            k = k_ref[pl.ds(k0, bs), :]
            v = v_ref[pl.ds(k0, bs), :]
            ka = jnp.where(kmine, k, jnp.where(kblk == n, 1.0, 0.0).astype(BF16))
            s = _dot_nt(qa, ka)
            s = s + slope * ((n - qi) * bs + col).astype(F32)
            if own:
                s = jnp.where(rel <= 0, s, NEG)
            m_new = jnp.maximum(m, jnp.max(s, axis=-1, keepdims=True))
            corr = jnp.exp(m - m_new)
            p = jnp.exp(s - m_new)
            l = l * corr + jnp.sum(p, axis=-1, keepdims=True)
            acc = acc * corr + _dot(p.astype(BF16), v)
            return m_new, l, acc

        carry = (jnp.full((tq, 1), NEG, F32), jnp.zeros((tq, 1), F32),
                 jnp.zeros((tq, 2 * MOBA_HEAD_DIM), F32))
        carry = step(qi, carry, True)
        carry = lax.fori_loop(0, qi, functools.partial(step, own=False), carry)
        _, l, acc = carry
        outs.append(acc / l)
    o_ref[...] = jnp.where(lane < MOBA_HEAD_DIM, outs[0], outs[1]).astype(BF16)


def _moba_prompt(q, k, v, means, slopes):
    s = q.shape[0]
    tq = MOBA_BLOCK
    assert means.shape[0] <= MOBA_HEAD_DIM
    means = jnp.pad(means, ((0, MOBA_HEAD_DIM - means.shape[0]), (0, 0)))
    nblk = MOBA_HEAD_DIM
    pairs = MOBA_HEADS // 2
    w = 2 * MOBA_HEAD_DIM
    return pl.pallas_call(
        _moba_prompt_kernel,
        out_shape=jax.ShapeDtypeStruct((s, MOBA_WIDTH), BF16),
        grid=(pairs, s // tq),
        in_specs=[pl.BlockSpec(memory_space=pltpu.SMEM),
                  pl.BlockSpec((tq, w), lambda p, i: (i, p)),
                  pl.BlockSpec((s, w), lambda p, i: (0, p)),
                  pl.BlockSpec((s, w), lambda p, i: (0, p)),
                  pl.BlockSpec((nblk, w), lambda p, i: (0, p))],
        out_specs=pl.BlockSpec((tq, w), lambda p, i: (i, p)),
        compiler_params=_params("arbitrary", "arbitrary"),
        name="moba_prompt",
    )(slopes, q, k, v, means)


def _mla_sample_kernel(pt_ref, q_ref, wk_ref, er_ref, wuv_ref, cnew_ref, knew_ref, *rest, pages):
    ckv_refs = rest[:pages]
    kr_refs = rest[pages:2 * pages]
    o_ref = rest[2 * pages]
    ql_ref, qr_ref, m_ref, l_ref, acc_ref = rest[2 * pages + 1:]
    c = pl.program_id(1)
    nh = MLA_HEADS

    @pl.when(c == 0)
    def _():
        q = q_ref[0]
        lane = lax.broadcasted_iota(I32, (nh, nh * HEAD_LANES), 1)
        row = lax.broadcasted_iota(I32, (nh, nh * HEAD_LANES), 0)
        qb = jnp.where((lane // HEAD_LANES) == row, q.astype(F32), 0.0).astype(BF16)
        ql_ref[...] = _dot(qb, wk_ref[...]).astype(BF16)
        qr_ref[...] = _dot(qb, er_ref[...]).astype(BF16)
        m_ref[...] = jnp.full(m_ref.shape, NEG, F32)
        l_ref[...] = jnp.zeros(l_ref.shape, F32)
        acc_ref[...] = jnp.zeros(acc_ref.shape, F32)

    ql = ql_ref[...]
    qr = qr_ref[...]
    ss = []
    cs = []
    for g in range(pages):
        ck = ckv_refs[g][0].astype(BF16)
        cs.append(ck)
        ss.append(_dot_nt(ql, ck) + _dot_nt(qr, kr_refs[g][0].astype(BF16)))
    s = jnp.concatenate(ss, axis=-1)
    m = m_ref[...]
    m_new = jnp.maximum(m, jnp.max(s, axis=-1, keepdims=True))
    corr = jnp.exp(m - m_new)
    p = jnp.exp(s - m_new)
    l_ref[...] = l_ref[...] * corr + jnp.sum(p, axis=-1, keepdims=True)
    pb = p.astype(BF16)
    upd = _dot(pb[:, :PAGE_SIZE], cs[0])
    for g in range(1, pages):
        upd = upd + _dot(pb[:, g * PAGE_SIZE:(g + 1) * PAGE_SIZE], cs[g])
    acc_ref[...] = acc_ref[...] * corr + upd
    m_ref[...] = m_new

    @pl.when(c == pl.num_programs(1) - 1)
    def _():
        cn = cnew_ref[0]
        cnb = cn.astype(BF16)
        sn = _dot_nt(ql, jnp.broadcast_to(cnb, (8, MLA_KV_LORA)))[:, :1] + \
            _dot_nt(qr, jnp.broadcast_to(knew_ref[0].astype(BF16), (8, MLA_ROPE)))[:, :1]
        m0 = m_ref[...]
        m1 = jnp.maximum(m0, sn)
        cr = jnp.exp(m0 - m1)
        pn = jnp.exp(sn - m1)
        l1 = l_ref[...] * cr + pn
        acc = acc_ref[...] * cr + pn.astype(BF16).astype(F32) * cnb.astype(F32)
        ol = (acc / l1).astype(BF16)
        full = _dot(ol, wuv_ref[...])
        lane = lax.broadcasted_iota(I32, full.shape, 1)
        row = lax.broadcasted_iota(I32, full.shape, 0)
        o_ref[0] = jnp.sum(jnp.where((lane // MLA_V) == row, full, 0.0), axis=0, keepdims=True)


def _mla_sample(q_s, wk, er, w_uv2, ckv_new, kr_new, ckv_pool, kr_pool, page_table):
    b, npg = page_table.shape
    pages = min(DECODE_PAGES, npg)
    hp = MLA_HEADS * HEAD_LANES

    def full(a):
        return pl.BlockSpec(a.shape, lambda i, c, pt: (0,) * a.ndim)

    def page_spec(width, g):
        return pl.BlockSpec((1, PAGE_SIZE, width), lambda i, c, pt: (pt[i, c * pages + g], 0, 0))

    in_specs = ([pl.BlockSpec((1, 1, hp), lambda i, c, pt: (i, 0, 0)), full(wk), full(er), full(w_uv2),
                 pl.BlockSpec((1, 1, MLA_KV_LORA), lambda i, c, pt: (i, 0, 0)),
                 pl.BlockSpec((1, 1, MLA_ROPE), lambda i, c, pt: (i, 0, 0))]
                + [page_spec(MLA_KV_LORA, g) for g in range(pages)]
                + [page_spec(MLA_ROPE, g) for g in range(pages)])
    grid_spec = pltpu.PrefetchScalarGridSpec(
        num_scalar_prefetch=1,
        grid=(b, npg // pages),
        in_specs=in_specs,
        out_specs=pl.BlockSpec((1, 1, MLA_HEADS * MLA_V), lambda i, c, pt: (i, 0, 0)),
        scratch_shapes=[pltpu.VMEM((MLA_HEADS, MLA_KV_LORA), BF16), pltpu.VMEM((MLA_HEADS, MLA_ROPE), BF16),
                        pltpu.VMEM((MLA_HEADS, 1), F32), pltpu.VMEM((MLA_HEADS, 1), F32),
                        pltpu.VMEM((MLA_HEADS, MLA_KV_LORA), F32)])
    out = pl.pallas_call(
        functools.partial(_mla_sample_kernel, pages=pages),
        out_shape=jax.ShapeDtypeStruct((b, 1, MLA_HEADS * MLA_V), F32),
        grid_spec=grid_spec,
        compiler_params=_params("arbitrary", "arbitrary"),
        name="mla_sample",
    )(page_table, q_s.reshape(b, 1, hp), wk, er, w_uv2, ckv_new.reshape(b, 1, MLA_KV_LORA),
      kr_new.reshape(b, 1, MLA_ROPE), *([ckv_pool] * pages), *([kr_pool] * pages))
    return out.reshape(b, MLA_HEADS * MLA_V)


def _moba_select_kernel(pt_ref, q_ref, *rest, pages):
    k_refs = rest[:pages]
    sel_ref = rest[pages]
    bsum_ref = rest[pages + 1]
    c = pl.program_id(1)
    blocks = pages // PAGES_PER_BLOCK
    for gb in range(blocks):
        tot = jnp.sum(k_refs[gb * PAGES_PER_BLOCK][0], axis=0, keepdims=True)
        for t in range(1, PAGES_PER_BLOCK):
            tot = tot + jnp.sum(k_refs[gb * PAGES_PER_BLOCK + t][0], axis=0, keepdims=True)
        bsum_ref[pl.ds(c * blocks + gb, 1), :] = tot

    @pl.when(c == pl.num_programs(1) - 1)
    def _():
        nblk = bsum_ref.shape[0]
        means = (bsum_ref[...] * (1.0 / MOBA_BLOCK)).astype(BF16)
        nh = MOBA_HEADS
        q = q_ref[0]
        lane = lax.broadcasted_iota(I32, (nh, MOBA_WIDTH), 1)
        row = lax.broadcasted_iota(I32, (nh, MOBA_WIDTH), 0)
        qb = jnp.where((lane // MOBA_HEAD_DIM) == row, q.astype(F32), 0.0).astype(BF16)
        gate = _dot_nt(qb, means)
        work = gate
        lanes = lax.broadcasted_iota(I32, gate.shape, 1)
        out_lane = lax.broadcasted_iota(I32, (nh, HEAD_LANES), 1)
        out = jnp.zeros((nh, HEAD_LANES), I32)
        for j in range(min(MOBA_TOPK, nblk)):
            mx = jnp.max(work, axis=-1, keepdims=True)
            first = jnp.min(jnp.where(work == mx, lanes, nblk), axis=-1, keepdims=True)
            out = jnp.where(out_lane == j, first, out)
            work = jnp.where(lanes == first, -jnp.inf, work)
        sel_ref[0] = out


def _moba_select(qm_s, k_pool, page_table):
    b, npg = page_table.shape
    pages = min(DECODE_PAGES, npg)

    def page_spec(g):
        return pl.BlockSpec((1, PAGE_SIZE, MOBA_WIDTH), lambda i, c, pt: (pt[i, c * pages + g], 0, 0))

    grid_spec = pltpu.PrefetchScalarGridSpec(
        num_scalar_prefetch=1,
        grid=(b, npg // pages),
        in_specs=[pl.BlockSpec((1, 1, MOBA_WIDTH), lambda i, c, pt: (i, 0, 0))]
        + [page_spec(g) for g in range(pages)],
        out_specs=pl.BlockSpec((1, MOBA_HEADS, HEAD_LANES), lambda i, c, pt: (i, 0, 0)),
        scratch_shapes=[pltpu.VMEM((npg // PAGES_PER_BLOCK, MOBA_WIDTH), F32)])
    sel = pl.pallas_call(
        functools.partial(_moba_select_kernel, pages=pages),
        out_shape=jax.ShapeDtypeStruct((b, MOBA_HEADS, HEAD_LANES), I32),
        grid_spec=grid_spec,
        compiler_params=_params("arbitrary", "arbitrary"),
        name="moba_select",
    )(page_table, qm_s.reshape(b, 1, MOBA_WIDTH), *([k_pool] * pages))
    return sel[:, :, :MOBA_TOPK]


def _moba_sample_kernel(pg_ref, sel_ref, slope_ref, q_ref, kn_ref, vn_ref, k_hbm, v_hbm, o_ref, kbuf, vbuf, sem,
                        *, n_sel, past):
    n_pg = 2 * n_sel * PAGES_PER_BLOCK
    i = pl.program_id(0)
    pr = pl.program_id(1)
    pairs = pl.num_programs(1)
    step = i * pairs + pr
    n_steps = pl.num_programs(0) * pairs
    slot = step % 2
    w = 2 * MOBA_HEAD_DIM

    def page_copies(stp, s, idx, page):
        lanes = pl.ds(pl.multiple_of((stp % pairs) * w, w), w)
        return (pltpu.make_async_copy(k_hbm.at[page, :, lanes], kbuf.at[s, idx], sem.at[0, s]),
                pltpu.make_async_copy(v_hbm.at[page, :, lanes], vbuf.at[s, idx], sem.at[1, s]))

    def fetch(stp, s):
        for idx in range(n_pg):
            for cp in page_copies(stp, s, idx, pg_ref[stp * n_pg + idx]):
                cp.start()

    @pl.when(step == 0)
    def _():
        fetch(0, 0)

    @pl.when(step + 1 < n_steps)
    def _():
        fetch(step + 1, 1 - slot)

    for idx in range(n_pg):
        for cp in page_copies(step, slot, idx, 0):
            cp.wait()

    lane8 = lax.broadcasted_iota(I32, (8, w), 1)
    row8 = lax.broadcasted_iota(I32, (8, w), 0)
    q = q_ref[0, 0]
    q8 = jnp.where((lane8 // MOBA_HEAD_DIM) == row8, q.astype(F32), 0.0).astype(BF16)
    kn = kn_ref[0, 0].astype(BF16)
    vn = vn_ref[0, 0].astype(BF16).astype(F32)
    s_new = _dot_nt(q8, jnp.broadcast_to(kn, (8, w)))[:, :1]
    col = lax.broadcasted_iota(I32, (1, MOBA_BLOCK), 1)
    lane1 = lax.broadcasted_iota(I32, (1, w), 1)
    outs = []
    for hh in range(2):
        slope = slope_ref[2 * pr + hh]
        ss = []
        vs = []
        for j in range(n_sel):
            base = (hh * n_sel + j) * PAGES_PER_BLOCK
            k = jnp.concatenate([kbuf[slot, base + t] for t in range(PAGES_PER_BLOCK)], axis=0).astype(BF16)
            vs.append(jnp.concatenate([vbuf[slot, base + t] for t in range(PAGES_PER_BLOCK)], axis=0).astype(BF16))
            blk = sel_ref[(i * MOBA_HEADS + 2 * pr + hh) * n_sel + j]
            dist = (past - blk * MOBA_BLOCK - col).astype(F32)
            ss.append(_dot_nt(q8, k)[hh:hh + 1, :] - slope * dist)
        sn = s_new[hh:hh + 1, :]
        m = sn
        for s in ss:
            m = jnp.maximum(m, jnp.max(s, axis=-1, keepdims=True))
        pn = jnp.exp(sn - m)
        l = pn
        acc = pn.astype(BF16).astype(F32) * vn
        for s, v in zip(ss, vs):
            p = jnp.exp(s - m)
            l = l + jnp.sum(p, axis=-1, keepdims=True)
            acc = acc + _dot(jnp.broadcast_to(p.astype(BF16), (8, MOBA_BLOCK)), v)[:1, :]
        outs.append(acc / l)
    o_ref[0, 0] = jnp.where(lane1 < MOBA_HEAD_DIM, outs[0], outs[1])


def _moba_sample(qm_s, km_s, vm_s, k_pool, v_pool, page_table, sel, slopes):
    b, npg = page_table.shape
    n_sel = sel.shape[-1]
    pairs = MOBA_HEADS // 2
    w = 2 * MOBA_HEAD_DIM
    past = npg * PAGE_SIZE
    ppb = PAGES_PER_BLOCK
    n_pg = 2 * n_sel * ppb
    logical = sel[..., None] * ppb + jnp.arange(ppb, dtype=I32)
    phys = jnp.take_along_axis(page_table, logical.reshape(b, -1), axis=1).reshape(-1)

    def tok_spec():
        return pl.BlockSpec((1, 1, 1, w), lambda i, p, pg, sl: (i, p, 0, 0))

    grid_spec = pltpu.PrefetchScalarGridSpec(
        num_scalar_prefetch=2,
        grid=(b, pairs),
        in_specs=[pl.BlockSpec(memory_space=pltpu.SMEM), tok_spec(), tok_spec(), tok_spec(),
                  pl.BlockSpec(memory_space=pl.ANY), pl.BlockSpec(memory_space=pl.ANY)],
        out_specs=tok_spec(),
        scratch_shapes=[pltpu.VMEM((2, n_pg, PAGE_SIZE, w), F32), pltpu.VMEM((2, n_pg, PAGE_SIZE, w), F32),
                        pltpu.SemaphoreType.DMA((2, 2))])
    out = pl.pallas_call(
        functools.partial(_moba_sample_kernel, n_sel=n_sel, past=past),
        out_shape=jax.ShapeDtypeStruct((b, pairs, 1, w), F32),
        grid_spec=grid_spec,
        compiler_params=_params("arbitrary", "arbitrary"),
        name="moba_sample",
    )(phys, sel.reshape(-1), slopes, qm_s.reshape(b, pairs, 1, w), km_s.reshape(b, pairs, 1, w),
      vm_s.reshape(b, pairs, 1, w), k_pool, v_pool)
    return out.reshape(b, MOBA_WIDTH)


def _post_kernel(x_ref, oa_ref, ob_ref, sga_ref, sgb_ref, gt1_ref, sh2_ref, sc2_ref, wba_ref, wbb_ref, wo_ref,
                 gpm_ref, gpf_ref, wr_ref, br_ref, x1_ref, h2_ref, te_ref, tw_ref):
    merged = (sga_ref[...].astype(F32) * _dot(oa_ref[...], wba_ref[...])
              + sgb_ref[...].astype(F32) * _dot(ob_ref[...], wbb_ref[...]))
    mix = _dot(merged.astype(BF16), wo_ref[...])
    x1 = x_ref[...] + gt1_ref[...] * _rms(mix, gpm_ref[...])
    x1_ref[...] = x1
    h2 = _rms(x1, gpf_ref[...]) * (1.0 + sc2_ref[...]) + sh2_ref[...]
    h2_ref[...] = h2
    logits = _dot(h2.astype(BF16), wr_ref[...]) + br_ref[...]
    ne = logits.shape[1]
    lanes = lax.broadcasted_iota(I32, logits.shape, 1)
    out_lane = lax.broadcasted_iota(I32, (logits.shape[0], TOP_K), 1)
    te = jnp.zeros((logits.shape[0], TOP_K), I32)
    tl = jnp.zeros((logits.shape[0], TOP_K), F32)
    work = logits
    for j in range(TOP_K):
        mx = jnp.max(work, axis=-1, keepdims=True)
        first = jnp.min(jnp.where(work == mx, lanes, ne), axis=-1, keepdims=True)
        te = jnp.where(out_lane == j, first, te)
        tl = jnp.where(out_lane == j, mx, tl)
        work = jnp.where(lanes == first, -jnp.inf, work)
    e = jnp.exp(tl - jnp.max(tl, axis=-1, keepdims=True))
    te_ref[...] = te
    tw_ref[...] = e / jnp.sum(e, axis=-1, keepdims=True)


def _post(x, o_a, o_b, sg, mod, wts):
    rows, d = x.shape
    tm = min(ROW_TILE, rows)
    mrows = mod.shape[0]
    mt = tm if mrows > 1 else 1
    mrow = (lambda i: i) if mrows > 1 else (lambda i: 0)

    def full(a):
        return pl.BlockSpec(a.shape, lambda i: (0,) * a.ndim)

    def rowblk(n, j=0):
        return pl.BlockSpec((tm, n), lambda i: (i, j))

    def modblk(j):
        return pl.BlockSpec((mt, d), lambda i: (mrow(i), j))

    names = ["w_br_mla", "w_br_moba", "w_out", "g_post_mix", "g_pre_ffn", "w_router", "b_router"]
    in_specs = ([rowblk(d), rowblk(MLA_HEADS * MLA_V), rowblk(MOBA_WIDTH), rowblk(d, 0), rowblk(d, 1),
                 modblk(2), modblk(3), modblk(4)] + [full(wts[n]) for n in names])
    return pl.pallas_call(
        _post_kernel,
        out_shape=[jax.ShapeDtypeStruct((rows, d), F32), jax.ShapeDtypeStruct((rows, d), F32),
                   jax.ShapeDtypeStruct((rows, TOP_K), I32), jax.ShapeDtypeStruct((rows, TOP_K), F32)],
        grid=(rows // tm,),
        in_specs=in_specs,
        out_specs=[rowblk(d), rowblk(d), rowblk(TOP_K), rowblk(TOP_K)],
        compiler_params=_params("arbitrary"),
        name="post",
    )(x, o_a, o_b, sg, sg, mod, mod, mod, *[wts[n] for n in names])


def _moe_kernel(be_ref, tor_ref, nu_ref, h_hbm, wg_ref, wu_ref, bg_ref, bu_ref, wd_ref, bd_ref, y_ref,
                xbuf, sem, *, rb):
    i = pl.program_id(0)
    n_used = nu_ref[0]
    slot = i % 2

    def gather_start(blk, s):
        def body(r, carry):
            tok = tor_ref[blk * rb + r]
            pltpu.make_async_copy(h_hbm.at[pl.ds(tok, 1), :], xbuf.at[s, pl.ds(r, 1), :], sem.at[s]).start()
            return carry
        lax.fori_loop(0, rb, body, 0)

    @pl.when(i == 0)
    def _():
        gather_start(0, 0)

    @pl.when(i + 1 < n_used)
    def _():
        gather_start(i + 1, 1 - slot)

    @pl.when(i < n_used)
    def _():
        pltpu.make_async_copy(h_hbm.at[pl.ds(0, rb), :], xbuf.at[slot], sem.at[slot]).wait()
        x = xbuf[slot].astype(BF16)
        g = jnp.minimum(_dot(x, wg_ref[0]) + bg_ref[0], SWIGLU_LIMIT)
        u = jnp.clip(_dot(x, wu_ref[0]) + bu_ref[0], -SWIGLU_LIMIT, SWIGLU_LIMIT)
        a = (u + 1.0) * (g * jax.nn.sigmoid(SWIGLU_ALPHA * g))
        y_ref[...] = _dot(a.astype(BF16), wd_ref[0]) + bd_ref[0]

    @pl.when(i >= n_used)
    def _():
        y_ref[...] = jnp.zeros(y_ref.shape, F32)


def _moe(h_all, blk_expert, token_of_row, n_used, wts):
    t, d = h_all.shape
    rb = MOE_ROWS
    n_blk = blk_expert.shape[0]
    dff = wts["w_g"].shape[2]

    def wspec(r, c):
        return pl.BlockSpec((1, r, c), lambda i, be, tor, nu: (be[i], 0, 0))

    grid_spec = pltpu.PrefetchScalarGridSpec(
        num_scalar_prefetch=3,
        grid=(n_blk,),
        in_specs=[pl.BlockSpec(memory_space=pl.ANY), wspec(d, dff), wspec(d, dff), wspec(1, dff), wspec(1, dff),
                  wspec(dff, d), wspec(1, d)],
        out_specs=pl.BlockSpec((rb, d), lambda i, be, tor, nu: (i, 0)),
        scratch_shapes=[pltpu.VMEM((2, rb, d), F32), pltpu.SemaphoreType.DMA((2,))])
    return pl.pallas_call(
        functools.partial(_moe_kernel, rb=rb),
        out_shape=jax.ShapeDtypeStruct((n_blk * rb, d), F32),
        grid_spec=grid_spec,
        compiler_params=_params("arbitrary"),
        name="moe",
    )(blk_expert, token_of_row, n_used, h_all, wts["w_g"], wts["w_u"], wts["b_g"], wts["b_u"],
      wts["w_d"], wts["b_d"])


def _combine_kernel(row_ref, y_hbm, tw_ref, x1_ref, gt2_ref, gpost_ref, o_ref, buf, sem, *, tm):
    i = pl.program_id(0)
    n = pl.num_programs(0)
    slot = i % 2

    def gather_start(tile, s):
        def body(r, carry):
            for kk in range(TOP_K):
                src = row_ref[(tile * tm + r) * TOP_K + kk]
                pltpu.make_async_copy(y_hbm.at[pl.ds(src, 1), :], buf.at[s, kk, pl.ds(r, 1), :], sem.at[s]).start()
            return carry
        lax.fori_loop(0, tm, body, 0)

    @pl.when(i == 0)
    def _():
        gather_start(0, 0)

    @pl.when(i + 1 < n)
    def _():
        gather_start(i + 1, 1 - slot)

    for kk in range(TOP_K):
        pltpu.make_async_copy(y_hbm.at[pl.ds(0, tm), :], buf.at[slot, kk], sem.at[slot]).wait()
    tw = tw_ref[...]
    y = tw[:, 0:1] * buf[slot, 0]
    for kk in range(1, TOP_K):
        y = y + tw[:, kk:kk + 1] * buf[slot, kk]
    o_ref[...] = x1_ref[...] + gt2_ref[...] * _rms(y, gpost_ref[...])


def _combine(y_rows, row, tw, x1, gt2, gpost):
    t, d = x1.shape
    tm = min(COMBINE_ROWS, t)
    assert t % tm == 0

    def rowblk(n):
        return pl.BlockSpec((tm, n), lambda i, r: (i, 0))

    grid_spec = pltpu.PrefetchScalarGridSpec(
        num_scalar_prefetch=1,
        grid=(t // tm,),
        in_specs=[pl.BlockSpec(memory_space=pl.ANY), rowblk(TOP_K), rowblk(d), rowblk(d),
                  pl.BlockSpec((1, d), lambda i, r: (0, 0))],
        out_specs=rowblk(d),
        scratch_shapes=[pltpu.VMEM((2, TOP_K, tm, d), F32), pltpu.SemaphoreType.DMA((2,))])
    return pl.pallas_call(
        functools.partial(_combine_kernel, tm=tm),
        out_shape=jax.ShapeDtypeStruct((t, d), F32),
        grid_spec=grid_spec,
        compiler_params=_params("arbitrary"),
        name="combine",
    )(row, y_rows, tw, x1, gt2, gpost)


def _rope_tables(pos):
    half = MLA_ROPE // 2
    inv_freq = ROPE_THETA ** (-jnp.arange(half, dtype=F32) / half)
    ang = pos.astype(F32)[:, None] * inv_freq[None, :]
    cos, sin = jnp.cos(ang), jnp.sin(ang)
    c32 = jnp.concatenate([cos, cos], axis=-1)
    s32 = jnp.concatenate([sin, sin], axis=-1)
    n = pos.shape[0]
    scale = MLA_QK ** -0.5
    pad = jnp.zeros((n, HEAD_LANES - MLA_QK), F32)
    tc = jnp.concatenate([jnp.full((n, MLA_NOPE), scale, F32), c32 * scale, pad], axis=-1)
    ts = jnp.concatenate([jnp.zeros((n, MLA_NOPE), F32), s32 * scale, pad], axis=-1)
    return tc, ts, c32, s32


def _swap_halves(w):
    half = w.shape[-1] // 2
    return jnp.concatenate([-w[..., half:], w[..., :half]], axis=-1)


def _prepare_weights(w_in, g_cq, w_uq, g_ckv, w_uk, w_uv, w_br_mla, w_br_moba, w_out, g_post_mix, g_pre_ffn,
                     w_router, b_router, w_gate_up, b_gate_up, w_down, b_down):
    d = w_in.shape[0]
    o_ckv = MLA_Q_LORA + MLA_KV_LORA
    o_kr = o_ckv + MLA_ROPE
    o_m = o_kr + 3 * MOBA_WIDTH
    w_kr = w_in[:, o_ckv:o_kr]
    uq = w_uq.reshape(MLA_Q_LORA, MLA_HEADS, MLA_QK)
    zq = jnp.zeros((MLA_Q_LORA, MLA_HEADS, HEAD_LANES - MLA_QK), F32)
    wq_main = jnp.concatenate([uq, zq], axis=-1).reshape(MLA_Q_LORA, -1)
    wq_swap = jnp.concatenate([jnp.zeros((MLA_Q_LORA, MLA_HEADS, MLA_NOPE), F32),
                               _swap_halves(uq[..., MLA_NOPE:]), zq], axis=-1).reshape(MLA_Q_LORA, -1)
    w_uk_pad = jnp.concatenate([w_uk, jnp.zeros((MLA_KV_LORA, MLA_HEADS, HEAD_LANES - MLA_NOPE), F32)],
                               axis=-1).reshape(MLA_KV_LORA, -1)
    eye = jnp.eye(MLA_ROPE, dtype=F32)
    e_head = jnp.concatenate([jnp.zeros((MLA_ROPE, MLA_NOPE), F32), eye,
                              jnp.zeros((MLA_ROPE, HEAD_LANES - MLA_QK), F32)], axis=-1)
    e_kr = jnp.tile(e_head, (1, MLA_HEADS))
    wk_abs = jnp.transpose(w_uk_pad.reshape(MLA_KV_LORA, -1))
    e_r = jnp.transpose(jnp.tile(e_head, (1, MLA_HEADS)))
    ne, _, two_f = w_gate_up.shape
    bf = lambda a: a.astype(BF16)
    return {
        "w_a": bf(w_in[:, :o_ckv]), "w_kr": bf(jnp.concatenate([w_kr, _swap_halves(w_kr)], axis=-1)),
        "w_m": bf(w_in[:, o_kr:o_m]), "w_g": bf(w_in[:, o_m:]),
        "g_cq": g_cq.reshape(1, -1), "g_ckv": g_ckv.reshape(1, -1),
        "wq_main": bf(wq_main), "wq_swap": bf(wq_swap), "w_uk_pad": bf(w_uk_pad), "e_kr": bf(e_kr),
        "w_uv2": bf(w_uv.reshape(MLA_KV_LORA, -1)), "wk_abs": bf(wk_abs), "e_r": bf(e_r),
        "w_br_mla": bf(w_br_mla), "w_br_moba": bf(w_br_moba), "w_out": bf(w_out),
        "g_post_mix": g_post_mix.reshape(1, -1), "g_pre_ffn": g_pre_ffn.reshape(1, -1),
        "w_router": bf(w_router), "b_router": b_router.reshape(1, -1),
        "w_g_e": bf(w_gate_up[:, :, 0::2]), "w_u_e": bf(w_gate_up[:, :, 1::2]),
        "b_g_e": b_gate_up[:, 0::2].reshape(ne, 1, two_f // 2), "b_u_e": b_gate_up[:, 1::2].reshape(ne, 1, two_f // 2),
        "w_d_e": bf(w_down), "b_d_e": b_down.reshape(ne, 1, d),
    }


def _moe_plan(top_e, rb):
    a = top_e.size
    flat_e = top_e.reshape(a)
    onehot = (flat_e[:, None] == jnp.arange(N_EXPERTS, dtype=I32)[None, :]).astype(I32)
    csum = jnp.cumsum(onehot, axis=0)
    counts = csum[-1]
    rank = jnp.take_along_axis(csum, flat_e[:, None], axis=1)[:, 0] - 1
    padded = (counts + rb - 1) // rb * rb
    pad_end = jnp.cumsum(padded)
    pad_start = pad_end - padded
    row = (pad_start[flat_e] + rank).astype(I32)
    n_blk = -(-a // rb) + N_EXPERTS
    token_of_row = jnp.zeros((n_blk * rb,), I32).at[row].set(jnp.arange(a, dtype=I32) // TOP_K)
    blk_expert = jnp.minimum(jnp.searchsorted(pad_end, jnp.arange(n_blk, dtype=I32) * rb, side="right"),
                             N_EXPERTS - 1).astype(I32)
    n_used = (pad_end[-1] // rb).astype(I32).reshape(1)
    return row, token_of_row, blk_expert, n_used


def kernel(x_prompt, x_sample, cache_mla_ckv, cache_mla_krope, cache_moba_k, cache_moba_v, page_table, c_prompt,
           c_sample, w_ada, b_ada, g_pre_mix, g_post_mix, g_pre_ffn, g_post_ffn, w_in, g_cq, w_uq, g_ckv, w_uk,
           w_uv, w_br_mla, w_br_moba, w_out, w_router, b_router, w_gate_up, b_gate_up, w_down, b_down):
    depth = w_in.shape[0]
    assert depth == 1 and x_prompt.shape[0] == 1 and x_sample.shape[1] == 1
    bp, sp, d = x_prompt.shape
    bs = x_sample.shape[0]
    n_pool = cache_mla_ckv.shape[1]
    npg = page_table.shape[1]
    assert npg % PAGES_PER_BLOCK == 0 and sp % MOBA_BLOCK == 0
    past = npg * PAGE_SIZE

    wts = _prepare_weights(w_in[0], g_cq[0], w_uq[0], g_ckv[0], w_uk[0], w_uv[0], w_br_mla[0], w_br_moba[0],
                           w_out[0], g_post_mix[0], g_pre_ffn[0], w_router[0], b_router[0], w_gate_up[0],
                           b_gate_up[0], w_down[0], b_down[0])
    slopes = 2.0 ** (-8.0 * jnp.arange(1, MOBA_HEADS + 1, dtype=F32) / MOBA_HEADS)

    c_all = jnp.concatenate([c_sample, c_prompt, jnp.zeros((7, d), F32)], axis=0)
    mod = _ada(c_all, w_ada[0], b_ada[0])
    mod_s, mod_p = mod[:bs], mod[bs:bs + 1]
    gpre = g_pre_mix[0].reshape(1, d)

    xp = x_prompt.reshape(sp, d)
    xs = x_sample.reshape(bs, d)
    tabs_p = _rope_tables(jnp.arange(sp))
    tabs_s = _rope_tables(jnp.full((bs,), past))

    (q_p, k_p, v_p, ckv_p, kr_p, qm_p, km_p, vm_p, kmb_p, vmb_p, sg_p, means_p) = _inproj(xp, mod_p, gpre, wts, tabs_p)
    (q_s, _, _, ckv_s, kr_s, qm_s, km_s, vm_s, _, _, sg_s, _) = _inproj(xs, mod_s, gpre, wts, tabs_s)

    oa_p = _mla_prompt(q_p, k_p, v_p)
    ob_p = _moba_prompt(qm_p, kmb_p, vmb_p, means_p.reshape(-1, MOBA_WIDTH), slopes)

    ckv_pool = cache_mla_ckv.reshape(n_pool, PAGE_SIZE, MLA_KV_LORA)
    kr_pool = cache_mla_krope.reshape(n_pool, PAGE_SIZE, MLA_ROPE)
    k_pool = cache_moba_k.reshape(n_pool, PAGE_SIZE, MOBA_WIDTH)
    v_pool = cache_moba_v.reshape(n_pool, PAGE_SIZE, MOBA_WIDTH)
    oa_s = _mla_sample(q_s, wts["wk_abs"], wts["e_r"], wts["w_uv2"], ckv_s, kr_s, ckv_pool, kr_pool, page_table)
    sel = _moba_select(qm_s, k_pool, page_table)
    ob_s = _moba_sample(qm_s, km_s, vm_s, k_pool, v_pool, page_table, sel, slopes)

    x1_p, h2_p, te_p, tw_p = _post(xp, oa_p, ob_p, sg_p, mod_p, wts)
    x1_s, h2_s, te_s, tw_s = _post(xs, oa_s.astype(BF16), ob_s.astype(BF16), sg_s, mod_s, wts)

    h_all = jnp.concatenate([h2_p, h2_s], axis=0)
    x1_all = jnp.concatenate([x1_p, x1_s], axis=0)
    te_all = jnp.concatenate([te_p, te_s], axis=0)
    tw_all = jnp.concatenate([tw_p, tw_s], axis=0)
    gt2_all = jnp.concatenate([jnp.broadcast_to(mod_p[:, 5 * d:], (sp, d)), mod_s[:, 5 * d:]], axis=0)
    row, token_of_row, blk_expert, n_used = _moe_plan(te_all, MOE_ROWS)
    ewts = {"w_g": wts["w_g_e"], "w_u": wts["w_u_e"], "b_g": wts["b_g_e"], "b_u": wts["b_u_e"],
            "w_d": wts["w_d_e"], "b_d": wts["b_d_e"]}
    y_rows = _moe(h_all, blk_expert, token_of_row, n_used, ewts)
    y_all = _combine(y_rows, row, tw_all, x1_all, gt2_all, g_post_ffn[0].reshape(1, d))

    hm, hd = MOBA_HEADS, MOBA_HEAD_DIM
    return (y_all[:sp].reshape(bp, sp, d), y_all[sp:].reshape(bs, 1, d),
            ckv_p.reshape(1, bp, sp, MLA_KV_LORA), kr_p.reshape(1, bp, sp, MLA_ROPE),
            km_p.reshape(1, bp, sp, hm, hd), vm_p.reshape(1, bp, sp, hm, hd),
            ckv_s.reshape(1, bs, 1, MLA_KV_LORA), kr_s.reshape(1, bs, 1, MLA_ROPE),
            km_s.reshape(1, bs, 1, hm, hd), vm_s.reshape(1, bs, 1, hm, hd))
```

```python
import functools

import jax
import jax.numpy as jnp
from jax import lax
from jax.experimental import pallas as pl
from jax.experimental.pallas import tpu as pltpu

F32 = jnp.float32
BF16 = jnp.bfloat16
I32 = jnp.int32

NORM_EPS = 1e-6
NEG = -1e30
ROPE_THETA = 10000.0

MLA_HEADS = 8
MLA_NOPE = 64
MLA_ROPE = 32
MLA_V = 64
MLA_QK = MLA_NOPE + MLA_ROPE
MLA_Q_LORA = 384
MLA_KV_LORA = 256
HEAD_LANES = 128
MOBA_HEADS = 8
MOBA_HEAD_DIM = 64
MOBA_BLOCK = 256
MOBA_TOPK = 3
MOBA_WIDTH = MOBA_HEADS * MOBA_HEAD_DIM
PAGE_SIZE = 128
PAGES_PER_BLOCK = MOBA_BLOCK // PAGE_SIZE
N_EXPERTS = 32
TOP_K = 4
SWIGLU_LIMIT = 7.0
SWIGLU_ALPHA = 1.702

ROW_TILE = 256
MLA_TQ = 512
MLA_TK = 2048
MOBA_TQ = 512
MOBA_TK = 2048
DECODE_PAGES = 8
DECODE_SEQS = 4
MOE_ROWS = 256
COMBINE_ROWS = 128
VMEM_LIMIT = 48 * 1024 * 1024

_NT = (((1,), (1,)), ((), ()))


def _rms(x, g):
    return x * lax.rsqrt(jnp.mean(x * x, axis=-1, keepdims=True) + NORM_EPS) * g


def _dot(a, b):
    return jnp.dot(a, b, preferred_element_type=F32)


def _dot_nt(a, b):
    return lax.dot_general(a, b, _NT, preferred_element_type=F32)


def _params(*sem):
    return pltpu.CompilerParams(dimension_semantics=sem, vmem_limit_bytes=VMEM_LIMIT)


def _ada_kernel(c_ref, w_ref, b_ref, o_ref):
    c = c_ref[...]
    s = (c * jax.nn.sigmoid(c)).astype(BF16)
    o_ref[...] = _dot(s, w_ref[...].astype(BF16)) + b_ref[...]


def _ada(c_all, w_ada, b_ada):
    rows, d = c_all.shape
    n = w_ada.shape[1]
    tn = 1024
    return pl.pallas_call(
        _ada_kernel,
        out_shape=jax.ShapeDtypeStruct((rows, n), F32),
        grid=(n // tn,),
        in_specs=[pl.BlockSpec((rows, d), lambda j: (0, 0)),
                  pl.BlockSpec((d, tn), lambda j: (0, j)),
                  pl.BlockSpec((1, tn), lambda j: (0, j))],
        out_specs=pl.BlockSpec((rows, tn), lambda j: (0, j)),
        compiler_params=_params("arbitrary"),
        name="ada",
    )(c_all, w_ada, b_ada.reshape(1, n))


def _inproj_kernel(x_ref, sh_ref, sc_ref, gpre_ref, wa_ref, wkr_ref, wm_ref, wg_ref, gcq_ref, gckv_ref,
                   wqm_ref, wqs_ref, tc_ref, ts_ref, wuk_ref, ekr_ref, wuv_ref, c32_ref, s32_ref,
                   q_ref, k_ref, v_ref, ckv_ref, kr_ref, qm_ref, km_ref, vm_ref, kmb_ref, vmb_ref,
                   sg_ref, mean_ref):
    x = x_ref[...]
    h = (_rms(x, gpre_ref[...]) * (1.0 + sc_ref[...]) + sh_ref[...]).astype(BF16)

    ya = _dot(h, wa_ref[...])
    cq = _rms(ya[:, :MLA_Q_LORA], gcq_ref[...]).astype(BF16)
    ckv = _rms(ya[:, MLA_Q_LORA:], gckv_ref[...])
    ckv_ref[...] = ckv
    ckv_b = ckv.astype(BF16)

    qmain = _dot(cq, wqm_ref[...])
    qswap = _dot(cq, wqs_ref[...])
    tc = tc_ref[...]
    ts = ts_ref[...]
    for hd in range(MLA_HEADS):
        sl = slice(hd * HEAD_LANES, (hd + 1) * HEAD_LANES)
        q_ref[:, sl] = (qmain[:, sl] * tc + qswap[:, sl] * ts).astype(BF16)

    ykr = _dot(h, wkr_ref[...])
    kr = ykr[:, :MLA_ROPE] * c32_ref[...] + ykr[:, MLA_ROPE:] * s32_ref[...]
    kr_ref[...] = kr
    k_ref[...] = (_dot(ckv_b, wuk_ref[...]) + _dot(kr.astype(BF16), ekr_ref[...])).astype(BF16)
    v_ref[...] = _dot(ckv_b, wuv_ref[...]).astype(BF16)

    ym = _dot(h, wm_ref[...])
    w = MOBA_WIDTH
    qm_ref[...] = (ym[:, :w] * (MOBA_HEAD_DIM ** -0.5)).astype(BF16)
    km = ym[:, w:2 * w]
    vm = ym[:, 2 * w:]
    km_ref[...] = km
    vm_ref[...] = vm
    kmb_ref[...] = km.astype(BF16)
    vmb_ref[...] = vm.astype(BF16)
    nb = mean_ref.shape[1]
    rows = km.shape[0] // nb
    for b in range(nb):
        mean_ref[0, b:b + 1, :] = jnp.mean(km[b * rows:(b + 1) * rows], axis=0, keepdims=True)

    sg_ref[...] = jax.nn.sigmoid(_dot(h, wg_ref[...])).astype(BF16)


def _inproj(x, mod, gpre, wts, tabs):
    rows, d = x.shape
    tm = min(ROW_TILE, rows)
    mrows = mod.shape[0]
    mt = tm if mrows > 1 else 1
    mrow = (lambda i: i) if mrows > 1 else (lambda i: 0)
    nb = max(tm // MOBA_BLOCK, 1)
    nt = rows // tm

    def full(a):
        return pl.BlockSpec(a.shape, lambda i: (0,) * a.ndim)

    def rowblk(n):
        return pl.BlockSpec((tm, n), lambda i: (i, 0))

    tc, ts, c32, s32 = tabs
    in_specs = [rowblk(d),
                pl.BlockSpec((mt, d), lambda i: (mrow(i), 0)),
                pl.BlockSpec((mt, d), lambda i: (mrow(i), 1)),
                full(gpre), full(wts["w_a"]), full(wts["w_kr"]), full(wts["w_m"]), full(wts["w_g"]),
                full(wts["g_cq"]), full(wts["g_ckv"]), full(wts["wq_main"]), full(wts["wq_swap"]),
                rowblk(HEAD_LANES), rowblk(HEAD_LANES), full(wts["w_uk_pad"]), full(wts["e_kr"]),
                full(wts["w_uv2"]), rowblk(MLA_ROPE), rowblk(MLA_ROPE)]
    hp = MLA_HEADS * HEAD_LANES
    out_shape = [jax.ShapeDtypeStruct((rows, hp), BF16),
                 jax.ShapeDtypeStruct((rows, hp), BF16),
                 jax.ShapeDtypeStruct((rows, MLA_HEADS * MLA_V), BF16),
                 jax.ShapeDtypeStruct((rows, MLA_KV_LORA), F32),
                 jax.ShapeDtypeStruct((rows, MLA_ROPE), F32),
                 jax.ShapeDtypeStruct((rows, MOBA_WIDTH), BF16),
                 jax.ShapeDtypeStruct((rows, MOBA_WIDTH), F32),
                 jax.ShapeDtypeStruct((rows, MOBA_WIDTH), F32),
                 jax.ShapeDtypeStruct((rows, MOBA_WIDTH), BF16),
                 jax.ShapeDtypeStruct((rows, MOBA_WIDTH), BF16),
                 jax.ShapeDtypeStruct((rows, 2 * d), BF16),
                 jax.ShapeDtypeStruct((nt, nb, MOBA_WIDTH), F32)]
    out_specs = [rowblk(hp), rowblk(hp), rowblk(MLA_HEADS * MLA_V), rowblk(MLA_KV_LORA), rowblk(MLA_ROPE),
                 rowblk(MOBA_WIDTH), rowblk(MOBA_WIDTH), rowblk(MOBA_WIDTH), rowblk(MOBA_WIDTH),
                 rowblk(MOBA_WIDTH), rowblk(2 * d),
                 pl.BlockSpec((1, nb, MOBA_WIDTH), lambda i: (i, 0, 0))]
    return pl.pallas_call(
        _inproj_kernel,
        out_shape=out_shape,
        grid=(nt,),
        in_specs=in_specs,
        out_specs=out_specs,
        compiler_params=_params("arbitrary"),
        name="inproj",
    )(x, mod, mod, gpre, wts["w_a"], wts["w_kr"], wts["w_m"], wts["w_g"], wts["g_cq"], wts["g_ckv"],
      wts["wq_main"], wts["wq_swap"], tc, ts, wts["w_uk_pad"], wts["e_kr"], wts["w_uv2"], c32, s32)


def _flash_update(s, v, m_ref, l_ref, acc_ref, hh):
    m = m_ref[hh]
    m_new = jnp.maximum(m, jnp.max(s, axis=-1, keepdims=True))
    corr = jnp.exp(m - m_new)
    p = jnp.exp(s - m_new)
    l_ref[hh] = l_ref[hh] * corr + jnp.sum(p, axis=-1, keepdims=True)
    acc_ref[hh] = acc_ref[hh] * corr + _dot(p.astype(BF16), v)
    m_ref[hh] = m_new


def _flash_init(m_ref, l_ref, acc_ref):
    m_ref[...] = jnp.full(m_ref.shape, NEG, F32)
    l_ref[...] = jnp.zeros(l_ref.shape, F32)
    acc_ref[...] = jnp.zeros(acc_ref.shape, F32)


def _mla_prompt_kernel(q_ref, k_ref, v_ref, o_ref, m_ref, l_ref, acc_ref, *, tq, tk):
    qi = pl.program_id(1)
    lane = lax.broadcasted_iota(I32, (tq, HEAD_LANES), 1)
    _flash_init(m_ref, l_ref, acc_ref)

    def step(k0, width, masked):
        v = v_ref[pl.ds(k0, width), :]
        for hh in range(2):
            hs = slice(hh * HEAD_LANES, (hh + 1) * HEAD_LANES)
            s = _dot_nt(q_ref[:, hs], k_ref[pl.ds(k0, width), hs])
            if masked:
                rel = lax.broadcasted_iota(I32, (tq, width), 1) - lax.broadcasted_iota(I32, (tq, width), 0)
                s = jnp.where(rel <= 0, s, NEG)
            _flash_update(s, v, m_ref, l_ref, acc_ref, hh)

    def wide_step(kb, carry):
        step(pl.multiple_of(kb * tk, tk), tk, False)
        return carry

    q0 = qi * tq
    nwide = q0 // tk
    lax.fori_loop(0, nwide, wide_step, 0)

    def narrow_step(j, carry):
        step(pl.multiple_of(nwide * tk + j * tq, tq), tq, False)
        return carry

    lax.fori_loop(0, (q0 - nwide * tk) // tq, narrow_step, 0)
    step(pl.multiple_of(q0, tq), tq, True)
    o = jnp.where(lane < MLA_V, acc_ref[0] / l_ref[0], acc_ref[1] / l_ref[1])
    o_ref[...] = o.astype(BF16)


def _mla_prompt(q, k, v):
    s = q.shape[0]
    tq = min(MLA_TQ, s)
    tk = min(MLA_TK, s)
    assert tk % tq == 0 and s % tk == 0
    pairs = MLA_HEADS // 2
    return pl.pallas_call(
        functools.partial(_mla_prompt_kernel, tq=tq, tk=tk),
        out_shape=jax.ShapeDtypeStruct((s, MLA_HEADS * MLA_V), BF16),
        grid=(pairs, s // tq),
        in_specs=[pl.BlockSpec((tq, 2 * HEAD_LANES), lambda p, i: (i, p)),
                  pl.BlockSpec((s, 2 * HEAD_LANES), lambda p, i: (0, p)),
                  pl.BlockSpec((s, 2 * MLA_V), lambda p, i: (0, p))],
        out_specs=pl.BlockSpec((tq, 2 * MLA_V), lambda p, i: (i, p)),
        scratch_shapes=[pltpu.VMEM((2, tq, 1), F32), pltpu.VMEM((2, tq, 1), F32),
                        pltpu.VMEM((2, tq, HEAD_LANES), F32)],
        compiler_params=_params("arbitrary", "arbitrary"),
        name="mla_prompt",
    )(q, k, v)


def _top3_penalty(gate, valid_lane, own_lane, n_elig):
    work = jnp.where(valid_lane, gate, -jnp.inf)
    lanes = lax.broadcasted_iota(I32, gate.shape, 1)
    pen = jnp.full(gate.shape, NEG, F32)
    for j in range(MOBA_TOPK):
        mx = jnp.max(work, axis=-1, keepdims=True)
        first = jnp.min(jnp.where(work == mx, lanes, 2 * gate.shape[1]), axis=-1, keepdims=True)
        pick = lanes == first
        pen = jnp.where(pick, jnp.where(j < n_elig, 0.0, NEG), pen)
        work = jnp.where(pick, -jnp.inf, work)
    return jnp.where(own_lane, 0.0, pen)


def _moba_prompt_kernel(slope_ref, q_ref, k_ref, v_ref, mean_ref, o_ref, qa_ref, m_ref, l_ref, acc_ref, *, tq, tk):
    pr = pl.program_id(0)
    qi = pl.program_id(1)
    bs = MOBA_BLOCK
    hd = MOBA_HEAD_DIM
    sub = tq // bs
    lane = lax.broadcasted_iota(I32, (tq, 2 * hd), 1)
    blk = lane & (hd - 1)
    own = qi * sub + lax.broadcasted_iota(I32, (tq, 2 * hd), 0) // bs
    q2 = q_ref[...]
    means = mean_ref[...].astype(BF16)
    means2 = jnp.concatenate([means, means], axis=0)
    for hh in range(2):
        mine = (lane < hd) if hh == 0 else (lane >= hd)
        qh = jnp.where(mine, q2, jnp.zeros_like(q2))
        gate = _dot_nt(qh, means2)
        gate = jnp.where(blk < own, gate, NEG)
        pen = _top3_penalty(gate, jnp.logical_not(mine), blk == own, own)
        qa_ref[hh] = jnp.where(mine, q2, pen.astype(BF16))
    _flash_init(m_ref, l_ref, acc_ref)
    q0 = qi * tq

    def step(k0, width, diagonal):
        k = k_ref[pl.ds(k0, width), :]
        v = v_ref[pl.ds(k0, width), :]
        klane = lax.broadcasted_iota(I32, (width, 2 * hd), 1)
        kblock = (k0 // bs) + lax.broadcasted_iota(I32, (width, 2 * hd), 0) // bs
        onehot = jnp.where((klane & (hd - 1)) == kblock, 1.0, 0.0).astype(BF16)
        col = lax.broadcasted_iota(I32, (1, width), 1)
        for hh in range(2):
            kmine = (klane < hd) if hh == 0 else (klane >= hd)
            s = _dot_nt(qa_ref[hh], jnp.where(kmine, k, onehot))
            s = s + slope_ref[2 * pr + hh] * (k0 - q0 + col).astype(F32)
            if diagonal:
                r = lax.broadcasted_iota(I32, (tq, width), 0)
                c = lax.broadcasted_iota(I32, (tq, width), 1)
                s = jnp.where(c // bs == r // bs, jnp.where(c > r, NEG, s), s)
            _flash_update(s, v, m_ref, l_ref, acc_ref, hh)

    step(pl.multiple_of(q0, tq), tq, True)

    def wide_step(kb, carry):
        step(pl.multiple_of(kb * tk, tk), tk, False)
        return carry

    nwide = q0 // tk
    lax.fori_loop(0, nwide, wide_step, 0)

    def narrow_step(j, carry):
        step(pl.multiple_of(nwide * tk + j * tq, tq), tq, False)
        return carry

    lax.fori_loop(0, (q0 - nwide * tk) // tq, narrow_step, 0)
    o = jnp.where(lane < hd, acc_ref[0] / l_ref[0], acc_ref[1] / l_ref[1])
    o_ref[...] = o.astype(BF16)


def _moba_prompt(q, k, v, means, slopes):
    s = q.shape[0]
    tq = min(MOBA_TQ, s)
    tk = min(MOBA_TK, s)
    assert tq % MOBA_BLOCK == 0 and tk % tq == 0 and s % tk == 0
    assert means.shape[0] <= MOBA_HEAD_DIM
    means = jnp.pad(means, ((0, MOBA_HEAD_DIM - means.shape[0]), (0, 0)))
    nblk = MOBA_HEAD_DIM
    pairs = MOBA_HEADS // 2
    w = 2 * MOBA_HEAD_DIM
    return pl.pallas_call(
        functools.partial(_moba_prompt_kernel, tq=tq, tk=tk),
        out_shape=jax.ShapeDtypeStruct((s, MOBA_WIDTH), BF16),
        grid=(pairs, s // tq),
        in_specs=[pl.BlockSpec(memory_space=pltpu.SMEM),
                  pl.BlockSpec((tq, w), lambda p, i: (i, p)),
                  pl.BlockSpec((s, w), lambda p, i: (0, p)),
                  pl.BlockSpec((s, w), lambda p, i: (0, p)),
                  pl.BlockSpec((nblk, w), lambda p, i: (0, p))],
        out_specs=pl.BlockSpec((tq, w), lambda p, i: (i, p)),
        scratch_shapes=[pltpu.VMEM((2, tq, w), BF16), pltpu.VMEM((2, tq, 1), F32), pltpu.VMEM((2, tq, 1), F32),
                        pltpu.VMEM((2, tq, w), F32)],
        compiler_params=_params("arbitrary", "arbitrary"),
        name="moba_prompt",
    )(slopes, q, k, v, means)


def _mla_sample_kernel(pt_ref, q_ref, wk_ref, er_ref, wuv_ref, cnew_ref, knew_ref, ckv_hbm, krt_hbm, o_ref,
                       cbuf, kbuf, sem, ql_ref, qr_ref, m_ref, l_ref, acc_ref, *, pages, nseq):
    i = pl.program_id(0)
    c = pl.program_id(1)
    nc = pl.num_programs(1)
    step = i * nc + c
    n_steps = pl.num_programs(0) * nc
    slot = step % 2
    nh = MLA_HEADS

    def page_copies(s, idx, page):
        return (pltpu.make_async_copy(ckv_hbm.at[page], cbuf.at[s, idx], sem.at[0, s]),
                pltpu.make_async_copy(krt_hbm.at[page], kbuf.at[s, idx], sem.at[1, s]))

    def fetch(ii, cc, s):
        for u in range(nseq):
            for g in range(pages):
                for cp in page_copies(s, u * pages + g, pt_ref[ii * nseq + u, cc * pages + g]):
                    cp.start()

    @pl.when(step == 0)
    def _():
        fetch(0, 0, 0)

    @pl.when(step + 1 < n_steps)
    def _():
        wrap = c + 1 == nc
        fetch(jnp.where(wrap, i + 1, i), jnp.where(wrap, 0, c + 1), 1 - slot)

    for idx in range(nseq * pages):
        for cp in page_copies(slot, idx, 0):
            cp.wait()

    @pl.when(c == 0)
    def _():
        lane = lax.broadcasted_iota(I32, (nh, nh * HEAD_LANES), 1)
        row = lax.broadcasted_iota(I32, (nh, nh * HEAD_LANES), 0)
        for u in range(nseq):
            q = q_ref[u]
            qb = jnp.where((lane // HEAD_LANES) == row, q.astype(F32), 0.0).astype(BF16)
            ql_ref[u] = _dot(qb, wk_ref[...]).astype(BF16)
            qr_ref[u] = _dot(qb, er_ref[...]).astype(BF16)
        m_ref[...] = jnp.full(m_ref.shape, NEG, F32)
        l_ref[...] = jnp.zeros(l_ref.shape, F32)
        acc_ref[...] = jnp.zeros(acc_ref.shape, F32)

    cks = [jnp.concatenate([cbuf[slot, u * pages + g].astype(BF16) for g in range(pages)], axis=0)
           for u in range(nseq)]
    ss = []
    for u in range(nseq):
        krt = jnp.concatenate([kbuf[slot, u * pages + g].astype(BF16) for g in range(pages)], axis=1)
        ss.append(_dot_nt(ql_ref[u], cks[u]) + _dot(qr_ref[u], krt))
    ps = []
    corrs = []
    for u in range(nseq):
        m = m_ref[u]
        m_new = jnp.maximum(m, jnp.max(ss[u], axis=-1, keepdims=True))
        corr = jnp.exp(m - m_new)
        p = jnp.exp(ss[u] - m_new)
        l_ref[u] = l_ref[u] * corr + jnp.sum(p, axis=-1, keepdims=True)
        m_ref[u] = m_new
        ps.append(p.astype(BF16))
        corrs.append(corr)
    for u in range(nseq):
        acc_ref[u] = acc_ref[u] * corrs[u] + _dot(ps[u], cks[u])

    @pl.when(c == pl.num_programs(1) - 1)
    def _():
        for u in range(nseq):
            ql = ql_ref[u]
            qr = qr_ref[u]
            cn = cnew_ref[u]
            cnb = cn.astype(BF16)
            sn = _dot_nt(ql, jnp.broadcast_to(cnb, (8, MLA_KV_LORA)))[:, :1] + \
                _dot_nt(qr, jnp.broadcast_to(knew_ref[u].astype(BF16), (8, MLA_ROPE)))[:, :1]
            m0 = m_ref[u]
            m1 = jnp.maximum(m0, sn)
            cr = jnp.exp(m0 - m1)
            pn = jnp.exp(sn - m1)
            l1 = l_ref[u] * cr + pn
            acc = acc_ref[u] * cr + pn.astype(BF16).astype(F32) * cnb.astype(F32)
            ol = (acc / l1).astype(BF16)
            full = _dot(ol, wuv_ref[...])
            lane = lax.broadcasted_iota(I32, full.shape, 1)
            row = lax.broadcasted_iota(I32, full.shape, 0)
            o_ref[u] = jnp.sum(jnp.where((lane // MLA_V) == row, full, 0.0), axis=0, keepdims=True)


def _mla_sample(q_s, wk, er, w_uv2, ckv_new, kr_new, ckv_pool, krt_pool, page_table):
    b, npg = page_table.shape
    pages = min(DECODE_PAGES, npg)
    nseq = DECODE_SEQS if b % DECODE_SEQS == 0 else 1
    hp = MLA_HEADS * HEAD_LANES

    def full(a):
        return pl.BlockSpec(a.shape, lambda i, c, pt: (0,) * a.ndim)

    def seq_spec(width):
        return pl.BlockSpec((nseq, 1, width), lambda i, c, pt: (i, 0, 0))

    n_pg = nseq * pages
    in_specs = [seq_spec(hp), full(wk), full(er), full(w_uv2), seq_spec(MLA_KV_LORA), seq_spec(MLA_ROPE),
                pl.BlockSpec(memory_space=pl.ANY), pl.BlockSpec(memory_space=pl.ANY)]
    grid_spec = pltpu.PrefetchScalarGridSpec(
        num_scalar_prefetch=1,
        grid=(b // nseq, npg // pages),
        in_specs=in_specs,
        out_specs=seq_spec(MLA_HEADS * MLA_V),
        scratch_shapes=[pltpu.VMEM((2, n_pg, PAGE_SIZE, MLA_KV_LORA), F32),
                        pltpu.VMEM((2, n_pg, MLA_ROPE, PAGE_SIZE), F32),
                        pltpu.SemaphoreType.DMA((2, 2)),
                        pltpu.VMEM((nseq, MLA_HEADS, MLA_KV_LORA), BF16), pltpu.VMEM((nseq, MLA_HEADS, MLA_ROPE), BF16),
                        pltpu.VMEM((nseq, MLA_HEADS, 1), F32), pltpu.VMEM((nseq, MLA_HEADS, 1), F32),
                        pltpu.VMEM((nseq, MLA_HEADS, MLA_KV_LORA), F32)])
    out = pl.pallas_call(
        functools.partial(_mla_sample_kernel, pages=pages, nseq=nseq),
        out_shape=jax.ShapeDtypeStruct((b, 1, MLA_HEADS * MLA_V), F32),
        grid_spec=grid_spec,
        compiler_params=_params("arbitrary", "arbitrary"),
        name="mla_sample",
    )(page_table, q_s.reshape(b, 1, hp), wk, er, w_uv2, ckv_new.reshape(b, 1, MLA_KV_LORA),
      kr_new.reshape(b, 1, MLA_ROPE), ckv_pool, krt_pool)
    return out.reshape(b, MLA_HEADS * MLA_V)


def _moba_select_kernel(pt_ref, q_ref, *rest, pages, nblk):
    k_refs = rest[:pages]
    sel_ref = rest[pages]
    bsum_ref = rest[pages + 1]
    c = pl.program_id(1)

    @pl.when(c == 0)
    def _():
        bsum_ref[...] = jnp.zeros(bsum_ref.shape, F32)

    blocks = pages // PAGES_PER_BLOCK
    lane = lax.broadcasted_iota(I32, bsum_ref.shape, 1)
    acc = bsum_ref[...]
    for gb in range(blocks):
        x = k_refs[gb * PAGES_PER_BLOCK][0]
        for t in range(1, PAGES_PER_BLOCK):
            x = x + k_refs[gb * PAGES_PER_BLOCK + t][0]
        acc = jnp.where(lane == c * blocks + gb, jnp.sum(x, axis=-1, keepdims=True), acc)
    bsum_ref[...] = acc

    @pl.when(c == pl.num_programs(1) - 1)
    def _():
        means_t = (bsum_ref[...] * (1.0 / MOBA_BLOCK)).astype(BF16)
        nh = MOBA_HEADS
        q = q_ref[0]
        hl = lax.broadcasted_iota(I32, (nh, MOBA_WIDTH), 1)
        row = lax.broadcasted_iota(I32, (nh, MOBA_WIDTH), 0)
        qb = jnp.where((hl // MOBA_HEAD_DIM) == row, q.astype(F32), 0.0).astype(BF16)
        gate = _dot(qb, means_t)
        lanes = lax.broadcasted_iota(I32, gate.shape, 1)
        work = jnp.where(lanes < nblk, gate, -jnp.inf)
        out = jnp.zeros(gate.shape, I32)
        for j in range(min(MOBA_TOPK, nblk)):
            mx = jnp.max(work, axis=-1, keepdims=True)
            first = jnp.min(jnp.where(work == mx, lanes, HEAD_LANES), axis=-1, keepdims=True)
            out = jnp.where(lanes == j, first, out)
            work = jnp.where(lanes == first, -jnp.inf, work)
        sel_ref[0] = out


def _moba_select(qm_s, kt_pool, page_table):
    b, npg = page_table.shape
    pages = min(DECODE_PAGES, npg)
    nblk = npg // PAGES_PER_BLOCK
    assert pages % PAGES_PER_BLOCK == 0 and nblk <= HEAD_LANES

    def page_spec(g):
        return pl.BlockSpec((1, MOBA_WIDTH, PAGE_SIZE), lambda i, c, pt: (pt[i, c * pages + g], 0, 0))

    grid_spec = pltpu.PrefetchScalarGridSpec(
        num_scalar_prefetch=1,
        grid=(b, npg // pages),
        in_specs=[pl.BlockSpec((1, 1, MOBA_WIDTH), lambda i, c, pt: (i, 0, 0))]
        + [page_spec(g) for g in range(pages)],
        out_specs=pl.BlockSpec((1, MOBA_HEADS, HEAD_LANES), lambda i, c, pt: (i, 0, 0)),
        scratch_shapes=[pltpu.VMEM((MOBA_WIDTH, HEAD_LANES), F32)])
    sel = pl.pallas_call(
        functools.partial(_moba_select_kernel, pages=pages, nblk=nblk),
        out_shape=jax.ShapeDtypeStruct((b, MOBA_HEADS, HEAD_LANES), I32),
        grid_spec=grid_spec,
        compiler_params=_params("arbitrary", "arbitrary"),
        name="moba_select",
    )(page_table, qm_s.reshape(b, 1, MOBA_WIDTH), *([kt_pool] * pages))
    return sel[:, :, :MOBA_TOPK]


def _moba_sample_kernel(pg_ref, sel_ref, slope_ref, q_ref, kn_ref, vn_ref, k_hbm, v_hbm, o_ref, kbuf, vbuf, sem,
                        *, n_sel, past):
    nh = MOBA_HEADS
    hd = MOBA_HEAD_DIM
    per_head = n_sel * PAGES_PER_BLOCK
    n_pg = nh * per_head
    i = pl.program_id(0)
    n_steps = pl.num_programs(0)
    slot = i % 2

    def slab_copies(s, idx, page):
        rows = pl.ds((idx // per_head) * hd, hd)
        return (pltpu.make_async_copy(k_hbm.at[page, rows, :], kbuf.at[s, idx], sem.at[0, s]),
                pltpu.make_async_copy(v_hbm.at[page, rows, :], vbuf.at[s, idx], sem.at[1, s]))

    def fetch(stp, s):
        for idx in range(n_pg):
            for cp in slab_copies(s, idx, pg_ref[stp * n_pg + idx]):
                cp.start()

    @pl.when(i == 0)
    def _():
        fetch(0, 0)

    @pl.when(i + 1 < n_steps)
    def _():
        fetch(i + 1, 1 - slot)

    for idx in range(n_pg):
        for cp in slab_copies(slot, idx, 0):
            cp.wait()

    q = q_ref[0]
    kn = kn_ref[0].astype(BF16)
    vn = vn_ref[0].astype(BF16).astype(F32)
    col = lax.broadcasted_iota(I32, (1, PAGE_SIZE), 1)
    for h in range(nh):
        q8 = jnp.broadcast_to(q[h:h + 1, :], (8, hd))
        slope = slope_ref[h]
        sn = _dot_nt(q8, jnp.broadcast_to(kn[h:h + 1, :], (8, hd)))[:1, :1]
        ss = []
        for j in range(n_sel):
            blk = sel_ref[(i * nh + h) * n_sel + j]
            for t in range(PAGES_PER_BLOCK):
                kt = kbuf[slot, h * per_head + j * PAGES_PER_BLOCK + t].astype(BF16)
                dist = (past - blk * MOBA_BLOCK - t * PAGE_SIZE - col).astype(F32)
                ss.append(_dot(q8, kt)[:1, :] - slope * dist)
        m = sn
        for s in ss:
            m = jnp.maximum(m, jnp.max(s, axis=-1, keepdims=True))
        pn = jnp.exp(sn - m)
        l = pn
        acc = pn.astype(BF16).astype(F32) * vn[h:h + 1, :]
        for idx, s in enumerate(ss):
            p = jnp.exp(s - m)
            l = l + jnp.sum(p, axis=-1, keepdims=True)
            vt = vbuf[slot, h * per_head + idx].astype(BF16)
            acc = acc + _dot_nt(jnp.broadcast_to(p.astype(BF16), (8, PAGE_SIZE)), vt)[:1, :]
        o_ref[0, h:h + 1, :] = acc / l


def _moba_sample(qm_s, km_s, vm_s, kt_pool, vt_pool, page_table, sel, slopes):
    b, npg = page_table.shape
    n_sel = sel.shape[-1]
    nh, hd = MOBA_HEADS, MOBA_HEAD_DIM
    past = npg * PAGE_SIZE
    ppb = PAGES_PER_BLOCK
    n_pg = nh * n_sel * ppb
    logical = sel[..., None] * ppb + jnp.arange(ppb, dtype=I32)
    phys = jnp.take_along_axis(page_table, logical.reshape(b, -1), axis=1).reshape(-1)

    def tok_spec():
        return pl.BlockSpec((1, nh, hd), lambda i, pg, sl: (i, 0, 0))

    grid_spec = pltpu.PrefetchScalarGridSpec(
        num_scalar_prefetch=2,
        grid=(b,),
        in_specs=[pl.BlockSpec(memory_space=pltpu.SMEM), tok_spec(), tok_spec(), tok_spec(),
                  pl.BlockSpec(memory_space=pl.ANY), pl.BlockSpec(memory_space=pl.ANY)],
        out_specs=tok_spec(),
        scratch_shapes=[pltpu.VMEM((2, n_pg, hd, PAGE_SIZE), F32), pltpu.VMEM((2, n_pg, hd, PAGE_SIZE), F32),
                        pltpu.SemaphoreType.DMA((2, 2))])
    out = pl.pallas_call(
        functools.partial(_moba_sample_kernel, n_sel=n_sel, past=past),
        out_shape=jax.ShapeDtypeStruct((b, nh, hd), F32),
        grid_spec=grid_spec,
        compiler_params=_params("arbitrary"),
        name="moba_sample",
    )(phys, sel.reshape(-1), slopes, qm_s.reshape(b, nh, hd), km_s.reshape(b, nh, hd), vm_s.reshape(b, nh, hd),
      kt_pool, vt_pool)
    return out.reshape(b, MOBA_WIDTH)


def _post_kernel(x_ref, oa_ref, ob_ref, sga_ref, sgb_ref, gt1_ref, sh2_ref, sc2_ref, wba_ref, wbb_ref, wo_ref,
                 gpm_ref, gpf_ref, wr_ref, br_ref, x1_ref, h2_ref, te_ref, tw_ref):
    merged = (sga_ref[...].astype(F32) * _dot(oa_ref[...], wba_ref[...])
              + sgb_ref[...].astype(F32) * _dot(ob_ref[...], wbb_ref[...]))
    mix = _dot(merged.astype(BF16), wo_ref[...])
    x1 = x_ref[...] + gt1_ref[...] * _rms(mix, gpm_ref[...])
    x1_ref[...] = x1
    h2 = _rms(x1, gpf_ref[...]) * (1.0 + sc2_ref[...]) + sh2_ref[...]
    h2_ref[...] = h2
    logits = _dot(h2.astype(BF16), wr_ref[...]) + br_ref[...]
    ne = logits.shape[1]
    lanes = lax.broadcasted_iota(I32, logits.shape, 1)
    out_lane = lax.broadcasted_iota(I32, (logits.shape[0], TOP_K), 1)
    te = jnp.zeros((logits.shape[0], TOP_K), I32)
    tl = jnp.zeros((logits.shape[0], TOP_K), F32)
    work = logits
    for j in range(TOP_K):
        mx = jnp.max(work, axis=-1, keepdims=True)
        first = jnp.min(jnp.where(work == mx, lanes, ne), axis=-1, keepdims=True)
        te = jnp.where(out_lane == j, first, te)
        tl = jnp.where(out_lane == j, mx, tl)
        work = jnp.where(lanes == first, -jnp.inf, work)
    e = jnp.exp(tl - jnp.max(tl, axis=-1, keepdims=True))
    te_ref[...] = te
    tw_ref[...] = e / jnp.sum(e, axis=-1, keepdims=True)


def _post(x, o_a, o_b, sg, mod, wts):
    rows, d = x.shape
    tm = min(ROW_TILE, rows)
    mrows = mod.shape[0]
    mt = tm if mrows > 1 else 1
    mrow = (lambda i: i) if mrows > 1 else (lambda i: 0)

    def full(a):
        return pl.BlockSpec(a.shape, lambda i: (0,) * a.ndim)

    def rowblk(n, j=0):
        return pl.BlockSpec((tm, n), lambda i: (i, j))

    def modblk(j):
        return pl.BlockSpec((mt, d), lambda i: (mrow(i), j))

    names = ["w_br_mla", "w_br_moba", "w_out", "g_post_mix", "g_pre_ffn", "w_router", "b_router"]
    in_specs = ([rowblk(d), rowblk(MLA_HEADS * MLA_V), rowblk(MOBA_WIDTH), rowblk(d, 0), rowblk(d, 1),
                 modblk(2), modblk(3), modblk(4)] + [full(wts[n]) for n in names])
    return pl.pallas_call(
        _post_kernel,
        out_shape=[jax.ShapeDtypeStruct((rows, d), F32), jax.ShapeDtypeStruct((rows, d), F32),
                   jax.ShapeDtypeStruct((rows, TOP_K), I32), jax.ShapeDtypeStruct((rows, TOP_K), F32)],
        grid=(rows // tm,),
        in_specs=in_specs,
        out_specs=[rowblk(d), rowblk(d), rowblk(TOP_K), rowblk(TOP_K)],
        compiler_params=_params("arbitrary"),
        name="post",
    )(x, o_a, o_b, sg, sg, mod, mod, mod, *[wts[n] for n in names])


def _deinterleave_kernel(w_ref, p_ref, g_ref, u_ref):
    half = g_ref.shape[2]
    y = _dot(w_ref[0].astype(BF16), p_ref[...])
    g_ref[0] = y[:, :half].astype(BF16)
    u_ref[0] = y[:, half:].astype(BF16)


def _deinterleave(w_gate_up):
    ne, d, two_f = w_gate_up.shape
    chunk = 512
    half = chunk // 2
    idx = jnp.arange(chunk, dtype=I32)
    dest = jnp.where(idx % 2 == 0, idx // 2, half + idx // 2)
    perm = (dest[:, None] == jnp.arange(chunk, dtype=I32)[None, :]).astype(BF16)
    out = jax.ShapeDtypeStruct((ne, d, two_f // 2), BF16)
    return pl.pallas_call(
        _deinterleave_kernel,
        out_shape=[out, out],
        grid=(ne, two_f // chunk),
        in_specs=[pl.BlockSpec((1, d, chunk), lambda e, c: (e, 0, c)),
                  pl.BlockSpec((chunk, chunk), lambda e, c: (0, 0))],
        out_specs=[pl.BlockSpec((1, d, half), lambda e, c: (e, 0, c)),
                   pl.BlockSpec((1, d, half), lambda e, c: (e, 0, c))],
        compiler_params=_params("arbitrary", "arbitrary"),
        name="deinterleave",
    )(w_gate_up, perm)


def _moe_kernel(be_ref, tor_ref, h_hbm, wg_ref, wu_ref, bg_ref, bu_ref, wd_ref, bd_ref, y_ref, xbuf, sem, *, rb):
    i = pl.program_id(0)
    n = pl.num_programs(0)
    slot = i % 2

    def gather_start(blk, s):
        for r in range(rb):
            tok = tor_ref[blk * rb + r]
            pltpu.make_async_copy(h_hbm.at[pl.ds(tok, 1), :], xbuf.at[s, pl.ds(r, 1), :], sem.at[s]).start()

    def gather_wait(s):
        pltpu.make_async_copy(h_hbm.at[pl.ds(0, rb), :], xbuf.at[s], sem.at[s]).wait()

    @pl.when(i == 0)
    def _():
        gather_start(0, 0)

    gather_wait(slot)
    x = xbuf[slot].astype(BF16)
    gather_start(jnp.minimum(i + 1, n - 1), 1 - slot)
    g = jnp.minimum(_dot(x, wg_ref[0]) + bg_ref[0], SWIGLU_LIMIT)
    u = jnp.clip(_dot(x, wu_ref[0]) + bu_ref[0], -SWIGLU_LIMIT, SWIGLU_LIMIT)
    a = (u + 1.0) * (g * jax.nn.sigmoid(SWIGLU_ALPHA * g))
    y_ref[...] = _dot(a.astype(BF16), wd_ref[0]) + bd_ref[0]

    @pl.when(i == n - 1)
    def _():
        gather_wait(1 - slot)


def _moe(h_all, blk_expert, token_of_row, wts):
    t, d = h_all.shape
    rb = MOE_ROWS
    n_blk = blk_expert.shape[0]
    dff = wts["w_g"].shape[2]

    def wspec(r, c):
        return pl.BlockSpec((1, r, c), lambda i, be, tor: (be[i], 0, 0))

    grid_spec = pltpu.PrefetchScalarGridSpec(
        num_scalar_prefetch=2,
        grid=(n_blk,),
        in_specs=[pl.BlockSpec(memory_space=pl.ANY), wspec(d, dff), wspec(d, dff), wspec(1, dff), wspec(1, dff),
                  wspec(dff, d), wspec(1, d)],
        out_specs=pl.BlockSpec((rb, d), lambda i, be, tor: (i, 0)),
        scratch_shapes=[pltpu.VMEM((2, rb, d), F32), pltpu.SemaphoreType.DMA((2,))])
    return pl.pallas_call(
        functools.partial(_moe_kernel, rb=rb),
        out_shape=jax.ShapeDtypeStruct((n_blk * rb, d), F32),
        grid_spec=grid_spec,
        compiler_params=_params("arbitrary"),
        name="moe",
    )(blk_expert, token_of_row, h_all, wts["w_g"], wts["w_u"], wts["b_g"], wts["b_u"], wts["w_d"], wts["b_d"])


def _combine_kernel(row_ref, y_hbm, tw_ref, x1_ref, gt2_ref, gpost_ref, o_ref, buf, sem, *, tm):
    i = pl.program_id(0)
    n = pl.num_programs(0)
    slot = i % 2

    def gather_start(tile, s):
        def body(r, carry):
            for kk in range(TOP_K):
                src = row_ref[(tile * tm + r) * TOP_K + kk]
                pltpu.make_async_copy(y_hbm.at[pl.ds(src, 1), :], buf.at[s, kk, pl.ds(r, 1), :], sem.at[s]).start()
            return carry
        lax.fori_loop(0, tm, body, 0)

    @pl.when(i == 0)
    def _():
        gather_start(0, 0)

    @pl.when(i + 1 < n)
    def _():
        gather_start(i + 1, 1 - slot)

    for kk in range(TOP_K):
        pltpu.make_async_copy(y_hbm.at[pl.ds(0, tm), :], buf.at[slot, kk], sem.at[slot]).wait()
    tw = tw_ref[...]
    y = tw[:, 0:1] * buf[slot, 0]
    for kk in range(1, TOP_K):
        y = y + tw[:, kk:kk + 1] * buf[slot, kk]
    o_ref[...] = x1_ref[...] + gt2_ref[...] * _rms(y, gpost_ref[...])


def _combine(y_rows, row, tw, x1, gt2, gpost):
    t, d = x1.shape
    tm = min(COMBINE_ROWS, t)
    assert t % tm == 0

    def rowblk(n):
        return pl.BlockSpec((tm, n), lambda i, r: (i, 0))

    grid_spec = pltpu.PrefetchScalarGridSpec(
        num_scalar_prefetch=1,
        grid=(t // tm,),
        in_specs=[pl.BlockSpec(memory_space=pl.ANY), rowblk(TOP_K), rowblk(d), rowblk(d),
                  pl.BlockSpec((1, d), lambda i, r: (0, 0))],
        out_specs=rowblk(d),
        scratch_shapes=[pltpu.VMEM((2, TOP_K, tm, d), F32), pltpu.SemaphoreType.DMA((2,))])
    return pl.pallas_call(
        functools.partial(_combine_kernel, tm=tm),
        out_shape=jax.ShapeDtypeStruct((t, d), F32),
        grid_spec=grid_spec,
        compiler_params=_params("arbitrary"),
        name="combine",
    )(row, y_rows, tw, x1, gt2, gpost)


def _rope_tables(pos):
    half = MLA_ROPE // 2
    inv_freq = ROPE_THETA ** (-jnp.arange(half, dtype=F32) / half)
    ang = pos.astype(F32)[:, None] * inv_freq[None, :]
    cos, sin = jnp.cos(ang), jnp.sin(ang)
    c32 = jnp.concatenate([cos, cos], axis=-1)
    s32 = jnp.concatenate([sin, sin], axis=-1)
    n = pos.shape[0]
    scale = MLA_QK ** -0.5
    pad = jnp.zeros((n, HEAD_LANES - MLA_QK), F32)
    tc = jnp.concatenate([jnp.full((n, MLA_NOPE), scale, F32), c32 * scale, pad], axis=-1)
    ts = jnp.concatenate([jnp.zeros((n, MLA_NOPE), F32), s32 * scale, pad], axis=-1)
    return tc, ts, c32, s32


def _swap_halves(w):
    half = w.shape[-1] // 2
    return jnp.concatenate([-w[..., half:], w[..., :half]], axis=-1)


def _prepare_weights(w_in, g_cq, w_uq, g_ckv, w_uk, w_uv, w_br_mla, w_br_moba, w_out, g_post_mix, g_pre_ffn,
                     w_router, b_router, b_gate_up, w_down, b_down):
    d = w_in.shape[0]
    o_ckv = MLA_Q_LORA + MLA_KV_LORA
    o_kr = o_ckv + MLA_ROPE
    o_m = o_kr + 3 * MOBA_WIDTH
    w_kr = w_in[:, o_ckv:o_kr]
    uq = w_uq.reshape(MLA_Q_LORA, MLA_HEADS, MLA_QK)
    zq = jnp.zeros((MLA_Q_LORA, MLA_HEADS, HEAD_LANES - MLA_QK), F32)
    wq_main = jnp.concatenate([uq, zq], axis=-1).reshape(MLA_Q_LORA, -1)
    wq_swap = jnp.concatenate([jnp.zeros((MLA_Q_LORA, MLA_HEADS, MLA_NOPE), F32),
                               _swap_halves(uq[..., MLA_NOPE:]), zq], axis=-1).reshape(MLA_Q_LORA, -1)
    w_uk_pad = jnp.concatenate([w_uk, jnp.zeros((MLA_KV_LORA, MLA_HEADS, HEAD_LANES - MLA_NOPE), F32)],
                               axis=-1).reshape(MLA_KV_LORA, -1)
    eye = jnp.eye(MLA_ROPE, dtype=F32)
    e_head = jnp.concatenate([jnp.zeros((MLA_ROPE, MLA_NOPE), F32), eye,
                              jnp.zeros((MLA_ROPE, HEAD_LANES - MLA_QK), F32)], axis=-1)
    e_kr = jnp.tile(e_head, (1, MLA_HEADS))
    wk_abs = jnp.transpose(w_uk_pad.reshape(MLA_KV_LORA, -1))
    e_r = jnp.transpose(jnp.tile(e_head, (1, MLA_HEADS)))
    ne, two_f = b_gate_up.shape
    bf = lambda a: a.astype(BF16)
    return {
        "w_a": bf(w_in[:, :o_ckv]), "w_kr": bf(jnp.concatenate([w_kr, _swap_halves(w_kr)], axis=-1)),
        "w_m": bf(w_in[:, o_kr:o_m]), "w_g": bf(w_in[:, o_m:]),
        "g_cq": g_cq.reshape(1, -1), "g_ckv": g_ckv.reshape(1, -1),
        "wq_main": bf(wq_main), "wq_swap": bf(wq_swap), "w_uk_pad": bf(w_uk_pad), "e_kr": bf(e_kr),
        "w_uv2": bf(w_uv.reshape(MLA_KV_LORA, -1)), "wk_abs": bf(wk_abs), "e_r": bf(e_r),
        "w_br_mla": bf(w_br_mla), "w_br_moba": bf(w_br_moba), "w_out": bf(w_out),
        "g_post_mix": g_post_mix.reshape(1, -1), "g_pre_ffn": g_pre_ffn.reshape(1, -1),
        "w_router": bf(w_router), "b_router": b_router.reshape(1, -1),
        "b_g_e": b_gate_up[:, 0::2].reshape(ne, 1, two_f // 2), "b_u_e": b_gate_up[:, 1::2].reshape(ne, 1, two_f // 2),
        "w_d_e": bf(w_down), "b_d_e": b_down.reshape(ne, 1, d),
    }


def _moe_plan(top_e, rb):
    a = top_e.size
    flat_e = top_e.reshape(a)
    onehot = (flat_e[:, None] == jnp.arange(N_EXPERTS, dtype=I32)[None, :]).astype(I32)
    csum = jnp.cumsum(onehot, axis=0)
    counts = csum[-1]
    rank = jnp.take_along_axis(csum, flat_e[:, None], axis=1)[:, 0] - 1
    padded = (counts + rb - 1) // rb * rb
    pad_end = jnp.cumsum(padded)
    pad_start = pad_end - padded
    row = (pad_start[flat_e] + rank).astype(I32)
    n_blk = -(-a // rb) + N_EXPERTS
    token_of_row = jnp.zeros((n_blk * rb,), I32).at[row].set(jnp.arange(a, dtype=I32) // TOP_K)
    blk_expert = jnp.minimum(jnp.searchsorted(pad_end, jnp.arange(n_blk, dtype=I32) * rb, side="right"),
                             N_EXPERTS - 1).astype(I32)
    return row, token_of_row, blk_expert


def kernel(x_prompt, x_sample, cache_mla_ckv, cache_mla_krope, cache_moba_k, cache_moba_v, page_table, c_prompt,
           c_sample, w_ada, b_ada, g_pre_mix, g_post_mix, g_pre_ffn, g_post_ffn, w_in, g_cq, w_uq, g_ckv, w_uk,
           w_uv, w_br_mla, w_br_moba, w_out, w_router, b_router, w_gate_up, b_gate_up, w_down, b_down):
    depth = w_in.shape[0]
    assert depth == 1 and x_prompt.shape[0] == 1 and x_sample.shape[1] == 1
    bp, sp, d = x_prompt.shape
    bs = x_sample.shape[0]
    n_pool = cache_mla_ckv.shape[1]
    npg = page_table.shape[1]
    assert npg % PAGES_PER_BLOCK == 0 and sp % MOBA_BLOCK == 0
    past = npg * PAGE_SIZE

    wts = _prepare_weights(w_in[0], g_cq[0], w_uq[0], g_ckv[0], w_uk[0], w_uv[0], w_br_mla[0], w_br_moba[0],
                           w_out[0], g_post_mix[0], g_pre_ffn[0], w_router[0], b_router[0], b_gate_up[0],
                           w_down[0], b_down[0])
    slopes = 2.0 ** (-8.0 * jnp.arange(1, MOBA_HEADS + 1, dtype=F32) / MOBA_HEADS)

    c_all = jnp.concatenate([c_sample, c_prompt, jnp.zeros((7, d), F32)], axis=0)
    mod = _ada(c_all, w_ada[0], b_ada[0])
    mod_s, mod_p = mod[:bs], mod[bs:bs + 1]
    gpre = g_pre_mix[0].reshape(1, d)

    xp = x_prompt.reshape(sp, d)
    xs = x_sample.reshape(bs, d)
    tabs_p = _rope_tables(jnp.arange(sp))
    tabs_s = _rope_tables(jnp.full((bs,), past))

    (q_p, k_p, v_p, ckv_p, kr_p, qm_p, km_p, vm_p, kmb_p, vmb_p, sg_p, means_p) = _inproj(xp, mod_p, gpre, wts, tabs_p)
    (q_s, _, _, ckv_s, kr_s, qm_s, km_s, vm_s, _, _, sg_s, _) = _inproj(xs, mod_s, gpre, wts, tabs_s)

    oa_p = _mla_prompt(q_p, k_p, v_p)
    ob_p = _moba_prompt(qm_p, kmb_p, vmb_p, means_p.reshape(-1, MOBA_WIDTH), slopes)

    ckv_pool = cache_mla_ckv.reshape(n_pool, PAGE_SIZE, MLA_KV_LORA)
    krt_pool = jnp.swapaxes(cache_mla_krope.reshape(n_pool, PAGE_SIZE, MLA_ROPE), 1, 2)
    kt_pool = jnp.transpose(cache_moba_k.reshape(n_pool, PAGE_SIZE, MOBA_HEADS, MOBA_HEAD_DIM),
                            (0, 2, 3, 1)).reshape(n_pool, MOBA_WIDTH, PAGE_SIZE)
    vt_pool = jnp.transpose(cache_moba_v.reshape(n_pool, PAGE_SIZE, MOBA_HEADS, MOBA_HEAD_DIM),
                            (0, 2, 3, 1)).reshape(n_pool, MOBA_WIDTH, PAGE_SIZE)
    oa_s = _mla_sample(q_s, wts["wk_abs"], wts["e_r"], wts["w_uv2"], ckv_s, kr_s, ckv_pool, krt_pool, page_table)
    sel = _moba_select(qm_s, kt_pool, page_table)
    ob_s = _moba_sample(qm_s, km_s, vm_s, kt_pool, vt_pool, page_table, sel, slopes)

    x1_p, h2_p, te_p, tw_p = _post(xp, oa_p, ob_p, sg_p, mod_p, wts)
    x1_s, h2_s, te_s, tw_s = _post(xs, oa_s.astype(BF16), ob_s.astype(BF16), sg_s, mod_s, wts)

    h_all = jnp.concatenate([h2_p, h2_s], axis=0)
    x1_all = jnp.concatenate([x1_p, x1_s], axis=0)
    te_all = jnp.concatenate([te_p, te_s], axis=0)
    tw_all = jnp.concatenate([tw_p, tw_s], axis=0)
    gt2_all = jnp.concatenate([jnp.broadcast_to(mod_p[:, 5 * d:], (sp, d)), mod_s[:, 5 * d:]], axis=0)
    row, token_of_row, blk_expert = _moe_plan(te_all, MOE_ROWS)
    w_g_e, w_u_e = _deinterleave(w_gate_up[0])
    ewts = {"w_g": w_g_e, "w_u": w_u_e, "b_g": wts["b_g_e"], "b_u": wts["b_u_e"],
            "w_d": wts["w_d_e"], "b_d": wts["b_d_e"]}
    y_rows = _moe(h_all, blk_expert, token_of_row, ewts)
    y_all = _combine(y_rows, row, tw_all, x1_all, gt2_all, g_post_ffn[0].reshape(1, d))

    hm, hd = MOBA_HEADS, MOBA_HEAD_DIM
    return (y_all[:sp].reshape(bp, sp, d), y_all[sp:].reshape(bs, 1, d),
            ckv_p.reshape(1, bp, sp, MLA_KV_LORA), kr_p.reshape(1, bp, sp, MLA_ROPE),
            km_p.reshape(1, bp, sp, hm, hd), vm_p.reshape(1, bp, sp, hm, hd),
            ckv_s.reshape(1, bs, 1, MLA_KV_LORA), kr_s.reshape(1, bs, 1, MLA_ROPE),
            km_s.reshape(1, bs, 1, hm, hd), vm_s.reshape(1, bs, 1, hm, hd))
```

```python
import functools

import jax
import jax.numpy as jnp
from jax import lax
from jax.experimental import pallas as pl
from jax.experimental.pallas import tpu as pltpu

F32 = jnp.float32
BF16 = jnp.bfloat16
I32 = jnp.int32

NORM_EPS = 1e-6
NEG = -1e30
ROPE_THETA = 10000.0

MLA_HEADS = 8
MLA_NOPE = 64
MLA_ROPE = 32
MLA_V = 64
MLA_QK = MLA_NOPE + MLA_ROPE
MLA_Q_LORA = 384
MLA_KV_LORA = 256
HEAD_LANES = 128
MOBA_HEADS = 8
MOBA_HEAD_DIM = 64
MOBA_BLOCK = 256
MOBA_TOPK = 3
MOBA_WIDTH = MOBA_HEADS * MOBA_HEAD_DIM
PAGE_SIZE = 128
PAGES_PER_BLOCK = MOBA_BLOCK // PAGE_SIZE
N_EXPERTS = 32
TOP_K = 4
SWIGLU_LIMIT = 7.0
SWIGLU_ALPHA = 1.702

ROW_TILE = 256
MLA_TQ = 1024
MLA_TK = 2048
MOBA_TQ = 1024
MOBA_TK = 2048
DECODE_PAGES = 8
SELECT_PAGES = 16
DECODE_SEQS = 4
MOE_ROWS = 256
COMBINE_ROWS = 128
VMEM_LIMIT = 48 * 1024 * 1024

_NT = (((1,), (1,)), ((), ()))


def _rms(x, g):
    return x * lax.rsqrt(jnp.mean(x * x, axis=-1, keepdims=True) + NORM_EPS) * g


def _dot(a, b):
    return jnp.dot(a, b, preferred_element_type=F32)


def _dot_nt(a, b):
    return lax.dot_general(a, b, _NT, preferred_element_type=F32)


def _rows_to_tiles(x, ref):
    for j in range(ref.shape[-2]):
        ref[:, j, :] = x[:, j * HEAD_LANES:(j + 1) * HEAD_LANES]


def _tiles_to_rows(ref_view):
    return jnp.concatenate([ref_view[:, j, :] for j in range(ref_view.shape[-2])], axis=1)


def _params(*sem):
    return pltpu.CompilerParams(dimension_semantics=sem, vmem_limit_bytes=VMEM_LIMIT)


def _ada_kernel(c_ref, w_ref, b_ref, o_ref):
    c = c_ref[...]
    s = (c * jax.nn.sigmoid(c)).astype(BF16)
    o_ref[...] = _dot(s, w_ref[...].astype(BF16)) + b_ref[...]


def _ada(c_all, w_ada, b_ada):
    rows, d = c_all.shape
    n = w_ada.shape[1]
    tn = 1024
    return pl.pallas_call(
        _ada_kernel,
        out_shape=jax.ShapeDtypeStruct((rows, n), F32),
        grid=(n // tn,),
        in_specs=[pl.BlockSpec((rows, d), lambda j: (0, 0)),
                  pl.BlockSpec((d, tn), lambda j: (0, j)),
                  pl.BlockSpec((1, tn), lambda j: (0, j))],
        out_specs=pl.BlockSpec((rows, tn), lambda j: (0, j)),
        compiler_params=_params("arbitrary"),
        name="ada",
    )(c_all, w_ada, b_ada.reshape(1, n))


def _inproj_kernel(x_ref, sh_ref, sc_ref, gpre_ref, wa_ref, wkr_ref, wm_ref, wg_ref, gcq_ref, gckv_ref,
                   wqm_ref, wqs_ref, tc_ref, ts_ref, wuk_ref, ekr_ref, wuv_ref, c32_ref, s32_ref,
                   q_ref, k_ref, v_ref, ckv_ref, kr_ref, qm_ref, km_ref, vm_ref, kmb_ref, vmb_ref,
                   sg_ref, mean_ref):
    x = x_ref[...]
    h = (_rms(x, gpre_ref[...]) * (1.0 + sc_ref[...]) + sh_ref[...]).astype(BF16)

    ya = _dot(h, wa_ref[...])
    cq = _rms(ya[:, :MLA_Q_LORA], gcq_ref[...]).astype(BF16)
    ckv = _rms(ya[:, MLA_Q_LORA:], gckv_ref[...])
    ckv_ref[...] = ckv
    ckv_b = ckv.astype(BF16)

    qmain = _dot(cq, wqm_ref[...])
    qswap = _dot(cq, wqs_ref[...])
    tc = tc_ref[...]
    ts = ts_ref[...]
    for hd in range(MLA_HEADS):
        sl = slice(hd * HEAD_LANES, (hd + 1) * HEAD_LANES)
        q_ref[:, sl] = (qmain[:, sl] * tc + qswap[:, sl] * ts).astype(BF16)

    ykr = _dot(h, wkr_ref[...])
    kr = ykr[:, :MLA_ROPE] * c32_ref[...] + ykr[:, MLA_ROPE:] * s32_ref[...]
    kr_ref[...] = kr
    k_ref[...] = (_dot(ckv_b, wuk_ref[...]) + _dot(kr.astype(BF16), ekr_ref[...])).astype(BF16)
    v_ref[...] = _dot(ckv_b, wuv_ref[...]).astype(BF16)

    ym = _dot(h, wm_ref[...])
    w = MOBA_WIDTH
    qm_ref[...] = (ym[:, :w] * (MOBA_HEAD_DIM ** -0.5)).astype(BF16)
    km = ym[:, w:2 * w]
    vm = ym[:, 2 * w:]
    km_ref[...] = km
    vm_ref[...] = vm
    kmb_ref[...] = km.astype(BF16)
    vmb_ref[...] = vm.astype(BF16)
    nb = mean_ref.shape[1]
    rows = km.shape[0] // nb
    for b in range(nb):
        mean_ref[0, b:b + 1, :] = jnp.mean(km[b * rows:(b + 1) * rows], axis=0, keepdims=True)

    sg_ref[...] = jax.nn.sigmoid(_dot(h, wg_ref[...])).astype(BF16)


def _inproj(x, mod, gpre, wts, tabs):
    rows, d = x.shape
    tm = min(ROW_TILE, rows)
    mrows = mod.shape[0]
    mt = tm if mrows > 1 else 1
    mrow = (lambda i: i) if mrows > 1 else (lambda i: 0)
    nb = max(tm // MOBA_BLOCK, 1)
    nt = rows // tm

    def full(a):
        return pl.BlockSpec(a.shape, lambda i: (0,) * a.ndim)

    def rowblk(n):
        return pl.BlockSpec((tm, n), lambda i: (i, 0))

    tc, ts, c32, s32 = tabs
    in_specs = [rowblk(d),
                pl.BlockSpec((mt, d), lambda i: (mrow(i), 0)),
                pl.BlockSpec((mt, d), lambda i: (mrow(i), 1)),
                full(gpre), full(wts["w_a"]), full(wts["w_kr"]), full(wts["w_m"]), full(wts["w_g"]),
                full(wts["g_cq"]), full(wts["g_ckv"]), full(wts["wq_main"]), full(wts["wq_swap"]),
                rowblk(HEAD_LANES), rowblk(HEAD_LANES), full(wts["w_uk_pad"]), full(wts["e_kr"]),
                full(wts["w_uv2"]), rowblk(MLA_ROPE), rowblk(MLA_ROPE)]
    hp = MLA_HEADS * HEAD_LANES
    out_shape = [jax.ShapeDtypeStruct((rows, hp), BF16),
                 jax.ShapeDtypeStruct((rows, hp), BF16),
                 jax.ShapeDtypeStruct((rows, MLA_HEADS * MLA_V), BF16),
                 jax.ShapeDtypeStruct((rows, MLA_KV_LORA), F32),
                 jax.ShapeDtypeStruct((rows, MLA_ROPE), F32),
                 jax.ShapeDtypeStruct((rows, MOBA_WIDTH), BF16),
                 jax.ShapeDtypeStruct((rows, MOBA_WIDTH), F32),
                 jax.ShapeDtypeStruct((rows, MOBA_WIDTH), F32),
                 jax.ShapeDtypeStruct((rows, MOBA_WIDTH), BF16),
                 jax.ShapeDtypeStruct((rows, MOBA_WIDTH), BF16),
                 jax.ShapeDtypeStruct((rows, 2 * d), BF16),
                 jax.ShapeDtypeStruct((nt, nb, MOBA_WIDTH), F32)]
    out_specs = [rowblk(hp), rowblk(hp), rowblk(MLA_HEADS * MLA_V), rowblk(MLA_KV_LORA), rowblk(MLA_ROPE),
                 rowblk(MOBA_WIDTH), rowblk(MOBA_WIDTH), rowblk(MOBA_WIDTH), rowblk(MOBA_WIDTH),
                 rowblk(MOBA_WIDTH), rowblk(2 * d),
                 pl.BlockSpec((1, nb, MOBA_WIDTH), lambda i: (i, 0, 0))]
    return pl.pallas_call(
        _inproj_kernel,
        out_shape=out_shape,
        grid=(nt,),
        in_specs=in_specs,
        out_specs=out_specs,
        compiler_params=_params("arbitrary"),
        name="inproj",
    )(x, mod, mod, gpre, wts["w_a"], wts["w_kr"], wts["w_m"], wts["w_g"], wts["g_cq"], wts["g_ckv"],
      wts["wq_main"], wts["wq_swap"], tc, ts, wts["w_uk_pad"], wts["e_kr"], wts["w_uv2"], c32, s32)


def _flash_update(s, v, m_ref, l_ref, acc_ref, hh):
    m = m_ref[hh]
    m_new = jnp.maximum(m, jnp.max(s, axis=-1, keepdims=True))
    corr = jnp.exp(m - m_new)
    p = jnp.exp(s - m_new)
    l_ref[hh] = l_ref[hh] * corr + jnp.sum(p, axis=-1, keepdims=True)
    acc_ref[hh] = acc_ref[hh] * corr + _dot(p.astype(BF16), v)
    m_ref[hh] = m_new


def _flash_init(m_ref, l_ref, acc_ref):
    m_ref[...] = jnp.full(m_ref.shape, NEG, F32)
    l_ref[...] = jnp.zeros(l_ref.shape, F32)
    acc_ref[...] = jnp.zeros(acc_ref.shape, F32)


def _mla_prompt_kernel(q_ref, k_ref, v_ref, o_ref, m_ref, l_ref, acc_ref, *, tq, tk):
    qi = pl.program_id(1)
    lane = lax.broadcasted_iota(I32, (tq, HEAD_LANES), 1)
    _flash_init(m_ref, l_ref, acc_ref)

    def step(k0, width, masked):
        v = v_ref[pl.ds(k0, width), :]
        for hh in range(2):
            hs = slice(hh * HEAD_LANES, (hh + 1) * HEAD_LANES)
            s = _dot_nt(q_ref[:, hs], k_ref[pl.ds(k0, width), hs])
            if masked:
                rel = lax.broadcasted_iota(I32, (tq, width), 1) - lax.broadcasted_iota(I32, (tq, width), 0)
                s = jnp.where(rel <= 0, s, NEG)
            _flash_update(s, v, m_ref, l_ref, acc_ref, hh)

    def wide_step(kb, carry):
        step(pl.multiple_of(kb * tk, tk), tk, False)
        return carry

    q0 = qi * tq
    nwide = q0 // tk
    lax.fori_loop(0, nwide, wide_step, 0)

    def narrow_step(j, carry):
        step(pl.multiple_of(nwide * tk + j * tq, tq), tq, False)
        return carry

    lax.fori_loop(0, (q0 - nwide * tk) // tq, narrow_step, 0)
    step(pl.multiple_of(q0, tq), tq, True)
    o = jnp.where(lane < MLA_V, acc_ref[0] / l_ref[0], acc_ref[1] / l_ref[1])
    o_ref[...] = o.astype(BF16)


def _mla_prompt(q, k, v):
    s = q.shape[0]
    tq = min(MLA_TQ, s)
    tk = min(MLA_TK, s)
    assert tk % tq == 0 and s % tk == 0
    pairs = MLA_HEADS // 2
    return pl.pallas_call(
        functools.partial(_mla_prompt_kernel, tq=tq, tk=tk),
        out_shape=jax.ShapeDtypeStruct((s, MLA_HEADS * MLA_V), BF16),
        grid=(pairs, s // tq),
        in_specs=[pl.BlockSpec((tq, 2 * HEAD_LANES), lambda p, i: (i, p)),
                  pl.BlockSpec((s, 2 * HEAD_LANES), lambda p, i: (0, p), pipeline_mode=pl.Buffered(1)),
                  pl.BlockSpec((s, 2 * MLA_V), lambda p, i: (0, p), pipeline_mode=pl.Buffered(1))],
        out_specs=pl.BlockSpec((tq, 2 * MLA_V), lambda p, i: (i, p)),
        scratch_shapes=[pltpu.VMEM((2, tq, 1), F32), pltpu.VMEM((2, tq, 1), F32),
                        pltpu.VMEM((2, tq, HEAD_LANES), F32)],
        compiler_params=_params("arbitrary", "arbitrary"),
        name="mla_prompt",
    )(q, k, v)


def _top3_penalty(gate, valid_lane, own_lane, n_elig):
    work = jnp.where(valid_lane, gate, -jnp.inf)
    lanes = lax.broadcasted_iota(I32, gate.shape, 1)
    pen = jnp.full(gate.shape, NEG, F32)
    for j in range(MOBA_TOPK):
        mx = jnp.max(work, axis=-1, keepdims=True)
        first = jnp.min(jnp.where(work == mx, lanes, 2 * gate.shape[1]), axis=-1, keepdims=True)
        pick = lanes == first
        pen = jnp.where(pick, jnp.where(j < n_elig, 0.0, NEG), pen)
        work = jnp.where(pick, -jnp.inf, work)
    return jnp.where(own_lane, 0.0, pen)


def _moba_prompt_kernel(slope_ref, q_ref, k_ref, v_ref, mean_ref, o_ref, qa_ref, m_ref, l_ref, acc_ref, *, tq, tk):
    pr = pl.program_id(0)
    qi = pl.program_id(1)
    bs = MOBA_BLOCK
    hd = MOBA_HEAD_DIM
    sub = tq // bs
    lane = lax.broadcasted_iota(I32, (tq, 2 * hd), 1)
    blk = lane & (hd - 1)
    own = qi * sub + lax.broadcasted_iota(I32, (tq, 2 * hd), 0) // bs
    q2 = q_ref[...]
    means = mean_ref[...].astype(BF16)
    means2 = jnp.concatenate([means, means], axis=0)
    for hh in range(2):
        mine = (lane < hd) if hh == 0 else (lane >= hd)
        qh = jnp.where(mine, q2, jnp.zeros_like(q2))
        gate = _dot_nt(qh, means2)
        gate = jnp.where(blk < own, gate, NEG)
        pen = _top3_penalty(gate, jnp.logical_not(mine), blk == own, own)
        qa_ref[hh] = jnp.where(mine, q2, pen.astype(BF16))
    _flash_init(m_ref, l_ref, acc_ref)
    q0 = qi * tq

    def step(k0, width, diagonal):
        k = k_ref[pl.ds(k0, width), :]
        v = v_ref[pl.ds(k0, width), :]
        klane = lax.broadcasted_iota(I32, (width, 2 * hd), 1)
        kblock = (k0 // bs) + lax.broadcasted_iota(I32, (width, 2 * hd), 0) // bs
        onehot = jnp.where((klane & (hd - 1)) == kblock, 1.0, 0.0).astype(BF16)
        col = lax.broadcasted_iota(I32, (1, width), 1)
        for hh in range(2):
            kmine = (klane < hd) if hh == 0 else (klane >= hd)
            s = _dot_nt(qa_ref[hh], jnp.where(kmine, k, onehot))
            s = s + slope_ref[2 * pr + hh] * (k0 - q0 + col).astype(F32)
            if diagonal:
                r = lax.broadcasted_iota(I32, (tq, width), 0)
                c = lax.broadcasted_iota(I32, (tq, width), 1)
                s = jnp.where(c // bs == r // bs, jnp.where(c > r, NEG, s), s)
            _flash_update(s, v, m_ref, l_ref, acc_ref, hh)

    step(pl.multiple_of(q0, tq), tq, True)

    def wide_step(kb, carry):
        step(pl.multiple_of(kb * tk, tk), tk, False)
        return carry

    nwide = q0 // tk
    lax.fori_loop(0, nwide, wide_step, 0)

    def narrow_step(j, carry):
        step(pl.multiple_of(nwide * tk + j * tq, tq), tq, False)
        return carry

    lax.fori_loop(0, (q0 - nwide * tk) // tq, narrow_step, 0)
    o = jnp.where(lane < hd, acc_ref[0] / l_ref[0], acc_ref[1] / l_ref[1])
    o_ref[...] = o.astype(BF16)


def _moba_prompt(q, k, v, means, slopes):
    s = q.shape[0]
    tq = min(MOBA_TQ, s)
    tk = min(MOBA_TK, s)
    assert tq % MOBA_BLOCK == 0 and tk % tq == 0 and s % tk == 0
    assert means.shape[0] <= MOBA_HEAD_DIM
    means = jnp.pad(means, ((0, MOBA_HEAD_DIM - means.shape[0]), (0, 0)))
    nblk = MOBA_HEAD_DIM
    pairs = MOBA_HEADS // 2
    w = 2 * MOBA_HEAD_DIM
    return pl.pallas_call(
        functools.partial(_moba_prompt_kernel, tq=tq, tk=tk),
        out_shape=jax.ShapeDtypeStruct((s, MOBA_WIDTH), BF16),
        grid=(pairs, s // tq),
        in_specs=[pl.BlockSpec(memory_space=pltpu.SMEM),
                  pl.BlockSpec((tq, w), lambda p, i: (i, p)),
                  pl.BlockSpec((s, w), lambda p, i: (0, p), pipeline_mode=pl.Buffered(1)),
                  pl.BlockSpec((s, w), lambda p, i: (0, p), pipeline_mode=pl.Buffered(1)),
                  pl.BlockSpec((nblk, w), lambda p, i: (0, p))],
        out_specs=pl.BlockSpec((tq, w), lambda p, i: (i, p)),
        scratch_shapes=[pltpu.VMEM((2, tq, w), BF16), pltpu.VMEM((2, tq, 1), F32), pltpu.VMEM((2, tq, 1), F32),
                        pltpu.VMEM((2, tq, w), F32)],
        compiler_params=_params("arbitrary", "arbitrary"),
        name="moba_prompt",
    )(slopes, q, k, v, means)


def _mla_sample_kernel(pt_ref, q_ref, wk_ref, er_ref, wuv_ref, cnew_ref, knew_ref, ckv_hbm, krt_hbm, o_ref,
                       cbuf, kbuf, sem, ql_ref, qr_ref, m_ref, l_ref, acc_ref, *, pages, nseq):
    i = pl.program_id(0)
    c = pl.program_id(1)
    nc = pl.num_programs(1)
    step = i * nc + c
    n_steps = pl.num_programs(0) * nc
    slot = step % 2
    nh = MLA_HEADS

    def page_copies(s, idx, page):
        return (pltpu.make_async_copy(ckv_hbm.at[page], cbuf.at[s, idx], sem.at[0, s]),
                pltpu.make_async_copy(krt_hbm.at[page], kbuf.at[s, idx], sem.at[1, s]))

    def fetch(ii, cc, s):
        for u in range(nseq):
            for g in range(pages):
                for cp in page_copies(s, u * pages + g, pt_ref[ii * nseq + u, cc * pages + g]):
                    cp.start()

    @pl.when(step == 0)
    def _():
        fetch(0, 0, 0)

    @pl.when(step + 1 < n_steps)
    def _():
        wrap = c + 1 == nc
        fetch(jnp.where(wrap, i + 1, i), jnp.where(wrap, 0, c + 1), 1 - slot)

    for idx in range(nseq * pages):
        for cp in page_copies(slot, idx, 0):
            cp.wait()

    @pl.when(c == 0)
    def _():
        lane = lax.broadcasted_iota(I32, (nh, nh * HEAD_LANES), 1)
        row = lax.broadcasted_iota(I32, (nh, nh * HEAD_LANES), 0)
        for u in range(nseq):
            q = q_ref[u]
            qb = jnp.where((lane // HEAD_LANES) == row, q.astype(F32), 0.0).astype(BF16)
            ql_ref[u] = _dot(qb, wk_ref[...]).astype(BF16)
            qr_ref[u] = _dot(qb, er_ref[...]).astype(BF16)
        m_ref[...] = jnp.full(m_ref.shape, NEG, F32)
        l_ref[...] = jnp.zeros(l_ref.shape, F32)
        acc_ref[...] = jnp.zeros(acc_ref.shape, F32)

    cks = [jnp.concatenate([cbuf[slot, u * pages + g].astype(BF16) for g in range(pages)], axis=0)
           for u in range(nseq)]
    ss = []
    for u in range(nseq):
        krt = jnp.concatenate([kbuf[slot, u * pages + g].astype(BF16) for g in range(pages)], axis=1)
        ss.append(_dot_nt(ql_ref[u], cks[u]) + _dot(qr_ref[u], krt))
    ps = []
    corrs = []
    for u in range(nseq):
        m = m_ref[u]
        m_new = jnp.maximum(m, jnp.max(ss[u], axis=-1, keepdims=True))
        corr = jnp.exp(m - m_new)
        p = jnp.exp(ss[u] - m_new)
        l_ref[u] = l_ref[u] * corr + jnp.sum(p, axis=-1, keepdims=True)
        m_ref[u] = m_new
        ps.append(p.astype(BF16))
        corrs.append(corr)
    for u in range(nseq):
        acc_ref[u] = acc_ref[u] * corrs[u] + _dot(ps[u], cks[u])

    @pl.when(c == pl.num_programs(1) - 1)
    def _():
        for u in range(nseq):
            ql = ql_ref[u]
            qr = qr_ref[u]
            cn = cnew_ref[u]
            cnb = cn.astype(BF16)
            sn = _dot_nt(ql, jnp.broadcast_to(cnb, (8, MLA_KV_LORA)))[:, :1] + \
                _dot_nt(qr, jnp.broadcast_to(knew_ref[u].astype(BF16), (8, MLA_ROPE)))[:, :1]
            m0 = m_ref[u]
            m1 = jnp.maximum(m0, sn)
            cr = jnp.exp(m0 - m1)
            pn = jnp.exp(sn - m1)
            l1 = l_ref[u] * cr + pn
            acc = acc_ref[u] * cr + pn.astype(BF16).astype(F32) * cnb.astype(F32)
            ol = (acc / l1).astype(BF16)
            full = _dot(ol, wuv_ref[...])
            lane = lax.broadcasted_iota(I32, full.shape, 1)
            row = lax.broadcasted_iota(I32, full.shape, 0)
            o_ref[u] = jnp.sum(jnp.where((lane // MLA_V) == row, full, 0.0), axis=0, keepdims=True)


def _mla_sample(q_s, wk, er, w_uv2, ckv_new, kr_new, ckv_pool, krt_pool, page_table):
    b, npg = page_table.shape
    pages = min(DECODE_PAGES, npg)
    nseq = DECODE_SEQS if b % DECODE_SEQS == 0 else 1
    hp = MLA_HEADS * HEAD_LANES

    def full(a):
        return pl.BlockSpec(a.shape, lambda i, c, pt: (0,) * a.ndim)

    def seq_spec(width):
        return pl.BlockSpec((nseq, 1, width), lambda i, c, pt: (i, 0, 0))

    n_pg = nseq * pages
    in_specs = [seq_spec(hp), full(wk), full(er), full(w_uv2), seq_spec(MLA_KV_LORA), seq_spec(MLA_ROPE),
                pl.BlockSpec(memory_space=pl.ANY), pl.BlockSpec(memory_space=pl.ANY)]
    grid_spec = pltpu.PrefetchScalarGridSpec(
        num_scalar_prefetch=1,
        grid=(b // nseq, npg // pages),
        in_specs=in_specs,
        out_specs=seq_spec(MLA_HEADS * MLA_V),
        scratch_shapes=[pltpu.VMEM((2, n_pg, PAGE_SIZE, MLA_KV_LORA), F32),
                        pltpu.VMEM((2, n_pg, MLA_ROPE, PAGE_SIZE), F32),
                        pltpu.SemaphoreType.DMA((2, 2)),
                        pltpu.VMEM((nseq, MLA_HEADS, MLA_KV_LORA), BF16), pltpu.VMEM((nseq, MLA_HEADS, MLA_ROPE), BF16),
                        pltpu.VMEM((nseq, MLA_HEADS, 1), F32), pltpu.VMEM((nseq, MLA_HEADS, 1), F32),
                        pltpu.VMEM((nseq, MLA_HEADS, MLA_KV_LORA), F32)])
    out = pl.pallas_call(
        functools.partial(_mla_sample_kernel, pages=pages, nseq=nseq),
        out_shape=jax.ShapeDtypeStruct((b, 1, MLA_HEADS * MLA_V), F32),
        grid_spec=grid_spec,
        compiler_params=_params("arbitrary", "arbitrary"),
        name="mla_sample",
    )(page_table, q_s.reshape(b, 1, hp), wk, er, w_uv2, ckv_new.reshape(b, 1, MLA_KV_LORA),
      kr_new.reshape(b, 1, MLA_ROPE), ckv_pool, krt_pool)
    return out.reshape(b, MLA_HEADS * MLA_V)


def _moba_select_kernel(pt_ref, q_ref, k_hbm, sel_ref, kbuf, sem, bsum_ref, *, pages, nblk):
    i = pl.program_id(0)
    c = pl.program_id(1)
    nc = pl.num_programs(1)
    step = i * nc + c
    n_steps = pl.num_programs(0) * nc
    slot = step % 2

    def page_copy(s, g, page):
        return pltpu.make_async_copy(k_hbm.at[page], kbuf.at[s, g], sem.at[s])

    def fetch(ii, cc, s):
        for g in range(pages):
            page_copy(s, g, pt_ref[ii, cc * pages + g]).start()

    @pl.when(step == 0)
    def _():
        fetch(0, 0, 0)

    @pl.when(step + 1 < n_steps)
    def _():
        wrap = c + 1 == nc
        fetch(jnp.where(wrap, i + 1, i), jnp.where(wrap, 0, c + 1), 1 - slot)

    for g in range(pages):
        page_copy(slot, g, 0).wait()

    @pl.when(c == 0)
    def _():
        bsum_ref[...] = jnp.zeros(bsum_ref.shape, F32)

    blocks = pages // PAGES_PER_BLOCK
    lane = lax.broadcasted_iota(I32, bsum_ref.shape, 1)
    acc = bsum_ref[...]
    for gb in range(blocks):
        x = kbuf[slot, gb * PAGES_PER_BLOCK]
        for t in range(1, PAGES_PER_BLOCK):
            x = x + kbuf[slot, gb * PAGES_PER_BLOCK + t]
        acc = jnp.where(lane == c * blocks + gb, jnp.sum(x, axis=-1, keepdims=True), acc)
    bsum_ref[...] = acc

    @pl.when(c == nc - 1)
    def _():
        means_t = (bsum_ref[...] * (1.0 / MOBA_BLOCK)).astype(BF16)
        nh = MOBA_HEADS
        q = q_ref[0]
        hl = lax.broadcasted_iota(I32, (nh, MOBA_WIDTH), 1)
        row = lax.broadcasted_iota(I32, (nh, MOBA_WIDTH), 0)
        qb = jnp.where((hl // MOBA_HEAD_DIM) == row, q.astype(F32), 0.0).astype(BF16)
        gate = _dot(qb, means_t)
        lanes = lax.broadcasted_iota(I32, gate.shape, 1)
        work = jnp.where(lanes < nblk, gate, -jnp.inf)
        out = jnp.zeros(gate.shape, I32)
        for j in range(min(MOBA_TOPK, nblk)):
            mx = jnp.max(work, axis=-1, keepdims=True)
            first = jnp.min(jnp.where(work == mx, lanes, HEAD_LANES), axis=-1, keepdims=True)
            out = jnp.where(lanes == j, first, out)
            work = jnp.where(lanes == first, -jnp.inf, work)
        sel_ref[0] = out


def _moba_select(qm_s, kt_pool, page_table):
    b, npg = page_table.shape
    pages = min(SELECT_PAGES, npg)
    nblk = npg // PAGES_PER_BLOCK
    assert pages % PAGES_PER_BLOCK == 0 and npg % pages == 0 and nblk <= HEAD_LANES
    grid_spec = pltpu.PrefetchScalarGridSpec(
        num_scalar_prefetch=1,
        grid=(b, npg // pages),
        in_specs=[pl.BlockSpec((1, 1, MOBA_WIDTH), lambda i, c, pt: (i, 0, 0)), pl.BlockSpec(memory_space=pl.ANY)],
        out_specs=pl.BlockSpec((1, MOBA_HEADS, HEAD_LANES), lambda i, c, pt: (i, 0, 0)),
        scratch_shapes=[pltpu.VMEM((2, pages, MOBA_WIDTH, PAGE_SIZE), F32), pltpu.SemaphoreType.DMA((2,)),
                        pltpu.VMEM((MOBA_WIDTH, HEAD_LANES), F32)])
    sel = pl.pallas_call(
        functools.partial(_moba_select_kernel, pages=pages, nblk=nblk),
        out_shape=jax.ShapeDtypeStruct((b, MOBA_HEADS, HEAD_LANES), I32),
        grid_spec=grid_spec,
        compiler_params=_params("arbitrary", "arbitrary"),
        name="moba_select",
    )(page_table, qm_s.reshape(b, 1, MOBA_WIDTH), kt_pool)
    return sel[:, :, :MOBA_TOPK]


def _moba_sample_kernel(pg_ref, sel_ref, slope_ref, q_ref, kn_ref, vn_ref, k_hbm, v_hbm, o_ref, kbuf, vbuf, sem,
                        *, n_sel, past):
    nh = MOBA_HEADS
    hd = MOBA_HEAD_DIM
    per_head = n_sel * PAGES_PER_BLOCK
    n_pg = nh * per_head
    i = pl.program_id(0)
    n_steps = pl.num_programs(0)
    slot = i % 2

    def slab_copies(s, idx, page):
        rows = pl.ds((idx // per_head) * hd, hd)
        return (pltpu.make_async_copy(k_hbm.at[page, rows, :], kbuf.at[s, idx], sem.at[0, s]),
                pltpu.make_async_copy(v_hbm.at[page, rows, :], vbuf.at[s, idx], sem.at[1, s]))

    def fetch(stp, s):
        for idx in range(n_pg):
            for cp in slab_copies(s, idx, pg_ref[stp * n_pg + idx]):
                cp.start()

    @pl.when(i == 0)
    def _():
        fetch(0, 0)

    @pl.when(i + 1 < n_steps)
    def _():
        fetch(i + 1, 1 - slot)

    for idx in range(n_pg):
        for cp in slab_copies(slot, idx, 0):
            cp.wait()

    q = q_ref[0]
    kn = kn_ref[0].astype(BF16)
    vn = vn_ref[0].astype(BF16).astype(F32)
    col = lax.broadcasted_iota(I32, (1, PAGE_SIZE), 1)
    for h in range(nh):
        q8 = jnp.broadcast_to(q[h:h + 1, :], (8, hd))
        slope = slope_ref[h]
        sn = _dot_nt(q8, jnp.broadcast_to(kn[h:h + 1, :], (8, hd)))[:1, :1]
        ss = []
        for j in range(n_sel):
            blk = sel_ref[(i * nh + h) * n_sel + j]
            for t in range(PAGES_PER_BLOCK):
                kt = kbuf[slot, h * per_head + j * PAGES_PER_BLOCK + t].astype(BF16)
                dist = (past - blk * MOBA_BLOCK - t * PAGE_SIZE - col).astype(F32)
                ss.append(_dot(q8, kt)[:1, :] - slope * dist)
        m = sn
        for s in ss:
            m = jnp.maximum(m, jnp.max(s, axis=-1, keepdims=True))
        pn = jnp.exp(sn - m)
        l = pn
        acc = pn.astype(BF16).astype(F32) * vn[h:h + 1, :]
        for idx, s in enumerate(ss):
            p = jnp.exp(s - m)
            l = l + jnp.sum(p, axis=-1, keepdims=True)
            vt = vbuf[slot, h * per_head + idx].astype(BF16)
            acc = acc + _dot_nt(jnp.broadcast_to(p.astype(BF16), (8, PAGE_SIZE)), vt)[:1, :]
        o_ref[0, h:h + 1, :] = acc / l


def _moba_sample(qm_s, km_s, vm_s, kt_pool, vt_pool, page_table, sel, slopes):
    b, npg = page_table.shape
    n_sel = sel.shape[-1]
    nh, hd = MOBA_HEADS, MOBA_HEAD_DIM
    past = npg * PAGE_SIZE
    ppb = PAGES_PER_BLOCK
    n_pg = nh * n_sel * ppb
    logical = sel[..., None] * ppb + jnp.arange(ppb, dtype=I32)
    phys = jnp.take_along_axis(page_table, logical.reshape(b, -1), axis=1).reshape(-1)

    def tok_spec():
        return pl.BlockSpec((1, nh, hd), lambda i, pg, sl: (i, 0, 0))

    grid_spec = pltpu.PrefetchScalarGridSpec(
        num_scalar_prefetch=2,
        grid=(b,),
        in_specs=[pl.BlockSpec(memory_space=pltpu.SMEM), tok_spec(), tok_spec(), tok_spec(),
                  pl.BlockSpec(memory_space=pl.ANY), pl.BlockSpec(memory_space=pl.ANY)],
        out_specs=tok_spec(),
        scratch_shapes=[pltpu.VMEM((2, n_pg, hd, PAGE_SIZE), F32), pltpu.VMEM((2, n_pg, hd, PAGE_SIZE), F32),
                        pltpu.SemaphoreType.DMA((2, 2))])
    out = pl.pallas_call(
        functools.partial(_moba_sample_kernel, n_sel=n_sel, past=past),
        out_shape=jax.ShapeDtypeStruct((b, nh, hd), F32),
        grid_spec=grid_spec,
        compiler_params=_params("arbitrary"),
        name="moba_sample",
    )(phys, sel.reshape(-1), slopes, qm_s.reshape(b, nh, hd), km_s.reshape(b, nh, hd), vm_s.reshape(b, nh, hd),
      kt_pool, vt_pool)
    return out.reshape(b, MOBA_WIDTH)


def _post_kernel(x_ref, oa_ref, ob_ref, sga_ref, sgb_ref, gt1_ref, sh2_ref, sc2_ref, wba_ref, wbb_ref, wo_ref,
                 gpm_ref, gpf_ref, wr_ref, br_ref, x1_ref, h2_ref, te_ref, tw_ref):
    merged = (sga_ref[...].astype(F32) * _dot(oa_ref[...], wba_ref[...])
              + sgb_ref[...].astype(F32) * _dot(ob_ref[...], wbb_ref[...]))
    mix = _dot(merged.astype(BF16), wo_ref[...])
    x1 = x_ref[...] + gt1_ref[...] * _rms(mix, gpm_ref[...])
    x1_ref[...] = x1
    h2 = _rms(x1, gpf_ref[...]) * (1.0 + sc2_ref[...]) + sh2_ref[...]
    _rows_to_tiles(h2, h2_ref)
    logits = _dot(h2.astype(BF16), wr_ref[...]) + br_ref[...]
    ne = logits.shape[1]
    lanes = lax.broadcasted_iota(I32, logits.shape, 1)
    out_lane = lax.broadcasted_iota(I32, (logits.shape[0], TOP_K), 1)
    te = jnp.zeros((logits.shape[0], TOP_K), I32)
    tl = jnp.zeros((logits.shape[0], TOP_K), F32)
    work = logits
    for j in range(TOP_K):
        mx = jnp.max(work, axis=-1, keepdims=True)
        first = jnp.min(jnp.where(work == mx, lanes, ne), axis=-1, keepdims=True)
        te = jnp.where(out_lane == j, first, te)
        tl = jnp.where(out_lane == j, mx, tl)
        work = jnp.where(lanes == first, -jnp.inf, work)
    e = jnp.exp(tl - jnp.max(tl, axis=-1, keepdims=True))
    te_ref[...] = te
    tw_ref[...] = e / jnp.sum(e, axis=-1, keepdims=True)


def _post(x, o_a, o_b, sg, mod, wts):
    rows, d = x.shape
    tm = min(ROW_TILE, rows)
    mrows = mod.shape[0]
    mt = tm if mrows > 1 else 1
    mrow = (lambda i: i) if mrows > 1 else (lambda i: 0)

    def full(a):
        return pl.BlockSpec(a.shape, lambda i: (0,) * a.ndim)

    def rowblk(n, j=0):
        return pl.BlockSpec((tm, n), lambda i: (i, j))

    def modblk(j):
        return pl.BlockSpec((mt, d), lambda i: (mrow(i), j))

    names = ["w_br_mla", "w_br_moba", "w_out", "g_post_mix", "g_pre_ffn", "w_router", "b_router"]
    in_specs = ([rowblk(d), rowblk(MLA_HEADS * MLA_V), rowblk(MOBA_WIDTH), rowblk(d, 0), rowblk(d, 1),
                 modblk(2), modblk(3), modblk(4)] + [full(wts[n]) for n in names])
    return pl.pallas_call(
        _post_kernel,
        out_shape=[jax.ShapeDtypeStruct((rows, d), F32), jax.ShapeDtypeStruct((rows, d // HEAD_LANES, HEAD_LANES), F32),
                   jax.ShapeDtypeStruct((rows, TOP_K), I32), jax.ShapeDtypeStruct((rows, TOP_K), F32)],
        grid=(rows // tm,),
        in_specs=in_specs,
        out_specs=[rowblk(d), pl.BlockSpec((tm, d // HEAD_LANES, HEAD_LANES), lambda i: (i, 0, 0)),
                   rowblk(TOP_K), rowblk(TOP_K)],
        compiler_params=_params("arbitrary"),
        name="post",
    )(x, o_a, o_b, sg, sg, mod, mod, mod, *[wts[n] for n in names])


def _deinterleave_kernel(w_ref, p_ref, g_ref, u_ref):
    half = g_ref.shape[2]
    y = _dot(w_ref[0].astype(BF16), p_ref[...])
    g_ref[0] = y[:, :half].astype(BF16)
    u_ref[0] = y[:, half:].astype(BF16)


def _deinterleave(w_gate_up):
    ne, d, two_f = w_gate_up.shape
    chunk = 512
    half = chunk // 2
    idx = jnp.arange(chunk, dtype=I32)
    dest = jnp.where(idx % 2 == 0, idx // 2, half + idx // 2)
    perm = (dest[:, None] == jnp.arange(chunk, dtype=I32)[None, :]).astype(BF16)
    out = jax.ShapeDtypeStruct((ne, d, two_f // 2), BF16)
    return pl.pallas_call(
        _deinterleave_kernel,
        out_shape=[out, out],
        grid=(ne, two_f // chunk),
        in_specs=[pl.BlockSpec((1, d, chunk), lambda e, c: (e, 0, c)),
                  pl.BlockSpec((chunk, chunk), lambda e, c: (0, 0))],
        out_specs=[pl.BlockSpec((1, d, half), lambda e, c: (e, 0, c)),
                   pl.BlockSpec((1, d, half), lambda e, c: (e, 0, c))],
        compiler_params=_params("arbitrary", "arbitrary"),
        name="deinterleave",
    )(w_gate_up, perm)


def _moe_kernel(be_ref, tor_ref, h_hbm, wg_ref, wu_ref, bg_ref, bu_ref, wd_ref, bd_ref, y_ref, xbuf, sem, *, rb):
    i = pl.program_id(0)
    n = pl.num_programs(0)
    slot = i % 2

    def gather_start(blk, s):
        for r in range(rb):
            tok = tor_ref[blk * rb + r]
            pltpu.make_async_copy(h_hbm.at[tok], xbuf.at[s, r], sem.at[s]).start(priority=r % 2)

    def gather_wait(s):
        pltpu.make_async_copy(h_hbm.at[pl.ds(0, rb)], xbuf.at[s], sem.at[s]).wait()

    @pl.when(i == 0)
    def _():
        gather_start(0, 0)

    gather_wait(slot)
    x = _tiles_to_rows(xbuf.at[slot]).astype(BF16)
    gather_start(jnp.minimum(i + 1, n - 1), 1 - slot)
    g = jnp.minimum(_dot(x, wg_ref[0]) + bg_ref[0], SWIGLU_LIMIT)
    u = jnp.clip(_dot(x, wu_ref[0]) + bu_ref[0], -SWIGLU_LIMIT, SWIGLU_LIMIT)
    a = (u + 1.0) * (g * jax.nn.sigmoid(SWIGLU_ALPHA * g))
    _rows_to_tiles(_dot(a.astype(BF16), wd_ref[0]) + bd_ref[0], y_ref)

    @pl.when(i == n - 1)
    def _():
        gather_wait(1 - slot)


def _moe(h_all, blk_expert, token_of_row, wts):
    t, nt, _ = h_all.shape
    d = nt * HEAD_LANES
    rb = MOE_ROWS
    n_blk = blk_expert.shape[0]
    dff = wts["w_g"].shape[2]

    def wspec(r, c):
        return pl.BlockSpec((1, r, c), lambda i, be, tor: (be[i], 0, 0))

    grid_spec = pltpu.PrefetchScalarGridSpec(
        num_scalar_prefetch=2,
        grid=(n_blk,),
        in_specs=[pl.BlockSpec(memory_space=pl.ANY), wspec(d, dff), wspec(d, dff), wspec(1, dff), wspec(1, dff),
                  wspec(dff, d), wspec(1, d)],
        out_specs=pl.BlockSpec((rb, nt, HEAD_LANES), lambda i, be, tor: (i, 0, 0)),
        scratch_shapes=[pltpu.VMEM((2, rb, nt, HEAD_LANES), F32), pltpu.SemaphoreType.DMA((2,))])
    return pl.pallas_call(
        functools.partial(_moe_kernel, rb=rb),
        out_shape=jax.ShapeDtypeStruct((n_blk * rb, nt, HEAD_LANES), F32),
        grid_spec=grid_spec,
        compiler_params=_params("arbitrary"),
        name="moe",
    )(blk_expert, token_of_row, h_all, wts["w_g"], wts["w_u"], wts["b_g"], wts["b_u"], wts["w_d"], wts["b_d"])


def _combine_kernel(row_ref, y_hbm, tw_ref, x1_ref, gt2_ref, gpost_ref, o_ref, buf, sem, *, tm):
    i = pl.program_id(0)
    n = pl.num_programs(0)
    slot = i % 2

    def gather_start(tile, s):
        def body(r, carry):
            for kk in range(TOP_K):
                src = row_ref[(tile * tm + r) * TOP_K + kk]
                pltpu.make_async_copy(y_hbm.at[src], buf.at[s, kk, r], sem.at[s]).start(priority=kk % 2)
            return carry
        lax.fori_loop(0, tm, body, 0)

    @pl.when(i == 0)
    def _():
        gather_start(0, 0)

    @pl.when(i + 1 < n)
    def _():
        gather_start(i + 1, 1 - slot)

    for kk in range(TOP_K):
        pltpu.make_async_copy(y_hbm.at[pl.ds(0, tm)], buf.at[slot, kk], sem.at[slot]).wait()
    tw = tw_ref[...]
    y = tw[:, 0:1] * _tiles_to_rows(buf.at[slot, 0])
    for kk in range(1, TOP_K):
        y = y + tw[:, kk:kk + 1] * _tiles_to_rows(buf.at[slot, kk])
    o_ref[...] = x1_ref[...] + gt2_ref[...] * _rms(y, gpost_ref[...])


def _combine(y_rows, row, tw, x1, gt2, gpost):
    t, d = x1.shape
    tm = min(COMBINE_ROWS, t)
    assert t % tm == 0

    def rowblk(n):
        return pl.BlockSpec((tm, n), lambda i, r: (i, 0))

    grid_spec = pltpu.PrefetchScalarGridSpec(
        num_scalar_prefetch=1,
        grid=(t // tm,),
        in_specs=[pl.BlockSpec(memory_space=pl.ANY), rowblk(TOP_K), rowblk(d), rowblk(d),
                  pl.BlockSpec((1, d), lambda i, r: (0, 0))],
        out_specs=rowblk(d),
        scratch_shapes=[pltpu.VMEM((2, TOP_K, tm, d // HEAD_LANES, HEAD_LANES), F32), pltpu.SemaphoreType.DMA((2,))])
    return pl.pallas_call(
        functools.partial(_combine_kernel, tm=tm),
        out_shape=jax.ShapeDtypeStruct((t, d), F32),
        grid_spec=grid_spec,
        compiler_params=_params("arbitrary"),
        name="combine",
    )(row, y_rows, tw, x1, gt2, gpost)


def _rope_tables(pos):
    half = MLA_ROPE // 2
    inv_freq = ROPE_THETA ** (-jnp.arange(half, dtype=F32) / half)
    ang = pos.astype(F32)[:, None] * inv_freq[None, :]
    cos, sin = jnp.cos(ang), jnp.sin(ang)
    c32 = jnp.concatenate([cos, cos], axis=-1)
    s32 = jnp.concatenate([sin, sin], axis=-1)
    n = pos.shape[0]
    scale = MLA_QK ** -0.5
    pad = jnp.zeros((n, HEAD_LANES - MLA_QK), F32)
    tc = jnp.concatenate([jnp.full((n, MLA_NOPE), scale, F32), c32 * scale, pad], axis=-1)
    ts = jnp.concatenate([jnp.zeros((n, MLA_NOPE), F32), s32 * scale, pad], axis=-1)
    return tc, ts, c32, s32


def _swap_halves(w):
    half = w.shape[-1] // 2
    return jnp.concatenate([-w[..., half:], w[..., :half]], axis=-1)


def _prepare_weights(w_in, g_cq, w_uq, g_ckv, w_uk, w_uv, w_br_mla, w_br_moba, w_out, g_post_mix, g_pre_ffn,
                     w_router, b_router, b_gate_up, w_down, b_down):
    d = w_in.shape[0]
    o_ckv = MLA_Q_LORA + MLA_KV_LORA
    o_kr = o_ckv + MLA_ROPE
    o_m = o_kr + 3 * MOBA_WIDTH
    w_kr = w_in[:, o_ckv:o_kr]
    uq = w_uq.reshape(MLA_Q_LORA, MLA_HEADS, MLA_QK)
    zq = jnp.zeros((MLA_Q_LORA, MLA_HEADS, HEAD_LANES - MLA_QK), F32)
    wq_main = jnp.concatenate([uq, zq], axis=-1).reshape(MLA_Q_LORA, -1)
    wq_swap = jnp.concatenate([jnp.zeros((MLA_Q_LORA, MLA_HEADS, MLA_NOPE), F32),
                               _swap_halves(uq[..., MLA_NOPE:]), zq], axis=-1).reshape(MLA_Q_LORA, -1)
    w_uk_pad = jnp.concatenate([w_uk, jnp.zeros((MLA_KV_LORA, MLA_HEADS, HEAD_LANES - MLA_NOPE), F32)],
                               axis=-1).reshape(MLA_KV_LORA, -1)
    eye = jnp.eye(MLA_ROPE, dtype=F32)
    e_head = jnp.concatenate([jnp.zeros((MLA_ROPE, MLA_NOPE), F32), eye,
                              jnp.zeros((MLA_ROPE, HEAD_LANES - MLA_QK), F32)], axis=-1)
    e_kr = jnp.tile(e_head, (1, MLA_HEADS))
    wk_abs = jnp.transpose(w_uk_pad.reshape(MLA_KV_LORA, -1))
    e_r = jnp.transpose(jnp.tile(e_head, (1, MLA_HEADS)))
    ne, two_f = b_gate_up.shape
    bf = lambda a: a.astype(BF16)
    return {
        "w_a": bf(w_in[:, :o_ckv]), "w_kr": bf(jnp.concatenate([w_kr, _swap_halves(w_kr)], axis=-1)),
        "w_m": bf(w_in[:, o_kr:o_m]), "w_g": bf(w_in[:, o_m:]),
        "g_cq": g_cq.reshape(1, -1), "g_ckv": g_ckv.reshape(1, -1),
        "wq_main": bf(wq_main), "wq_swap": bf(wq_swap), "w_uk_pad": bf(w_uk_pad), "e_kr": bf(e_kr),
        "w_uv2": bf(w_uv.reshape(MLA_KV_LORA, -1)), "wk_abs": bf(wk_abs), "e_r": bf(e_r),
        "w_br_mla": bf(w_br_mla), "w_br_moba": bf(w_br_moba), "w_out": bf(w_out),
        "g_post_mix": g_post_mix.reshape(1, -1), "g_pre_ffn": g_pre_ffn.reshape(1, -1),
        "w_router": bf(w_router), "b_router": b_router.reshape(1, -1),
        "b_g_e": b_gate_up[:, 0::2].reshape(ne, 1, two_f // 2), "b_u_e": b_gate_up[:, 1::2].reshape(ne, 1, two_f // 2),
        "w_d_e": bf(w_down), "b_d_e": b_down.reshape(ne, 1, d),
    }


def _moe_plan(top_e, rb):
    a = top_e.size
    flat_e = top_e.reshape(a)
    onehot = flat_e[:, None] == jnp.arange(N_EXPERTS, dtype=I32)[None, :]
    chunk = 256
    assert a % chunk == 0
    oh = onehot.astype(BF16).reshape(a // chunk, chunk, N_EXPERTS)
    tri = (jnp.arange(chunk)[:, None] > jnp.arange(chunk)[None, :]).astype(BF16)
    within = jnp.einsum("ij,cje->cie", tri, oh, preferred_element_type=F32)
    totals = jnp.sum(oh.astype(F32), axis=1)
    before = jnp.cumsum(totals, axis=0) - totals
    counts = jnp.sum(totals, axis=0).astype(I32)
    rank_all = (within + before[:, None, :]).reshape(a, N_EXPERTS)
    rank = jnp.sum(jnp.where(onehot, rank_all, 0.0), axis=1).astype(I32)
    padded = (counts + rb - 1) // rb * rb
    pad_end = jnp.cumsum(padded)
    pad_start = pad_end - padded
    row = (jnp.sum(jnp.where(onehot, pad_start[None, :], 0), axis=1) + rank).astype(I32)
    n_blk = -(-a // rb) + N_EXPERTS
    token_of_row = jnp.zeros((n_blk * rb,), I32).at[row].set(jnp.arange(a, dtype=I32) // TOP_K)
    blk_start = jnp.arange(n_blk, dtype=I32) * rb
    blk_expert = jnp.minimum(jnp.sum((blk_start[:, None] >= pad_end[None, :]).astype(I32), axis=1), N_EXPERTS - 1)
    return row, token_of_row, blk_expert


def kernel(x_prompt, x_sample, cache_mla_ckv, cache_mla_krope, cache_moba_k, cache_moba_v, page_table, c_prompt,
           c_sample, w_ada, b_ada, g_pre_mix, g_post_mix, g_pre_ffn, g_post_ffn, w_in, g_cq, w_uq, g_ckv, w_uk,
           w_uv, w_br_mla, w_br_moba, w_out, w_router, b_router, w_gate_up, b_gate_up, w_down, b_down):
    depth = w_in.shape[0]
    assert depth == 1 and x_prompt.shape[0] == 1 and x_sample.shape[1] == 1
    bp, sp, d = x_prompt.shape
    bs = x_sample.shape[0]
    n_pool = cache_mla_ckv.shape[1]
    npg = page_table.shape[1]
    assert npg % PAGES_PER_BLOCK == 0 and sp % MOBA_BLOCK == 0
    past = npg * PAGE_SIZE

    wts = _prepare_weights(w_in[0], g_cq[0], w_uq[0], g_ckv[0], w_uk[0], w_uv[0], w_br_mla[0], w_br_moba[0],
                           w_out[0], g_post_mix[0], g_pre_ffn[0], w_router[0], b_router[0], b_gate_up[0],
                           w_down[0], b_down[0])
    slopes = 2.0 ** (-8.0 * jnp.arange(1, MOBA_HEADS + 1, dtype=F32) / MOBA_HEADS)

    c_all = jnp.concatenate([c_sample, c_prompt, jnp.zeros((7, d), F32)], axis=0)
    mod = _ada(c_all, w_ada[0], b_ada[0])
    mod_s, mod_p = mod[:bs], mod[bs:bs + 1]
    gpre = g_pre_mix[0].reshape(1, d)

    xp = x_prompt.reshape(sp, d)
    xs = x_sample.reshape(bs, d)
    tabs_p = _rope_tables(jnp.arange(sp))
    tabs_s = _rope_tables(jnp.full((bs,), past))

    (q_p, k_p, v_p, ckv_p, kr_p, qm_p, km_p, vm_p, kmb_p, vmb_p, sg_p, means_p) = _inproj(xp, mod_p, gpre, wts, tabs_p)
    (q_s, _, _, ckv_s, kr_s, qm_s, km_s, vm_s, _, _, sg_s, _) = _inproj(xs, mod_s, gpre, wts, tabs_s)

    oa_p = _mla_prompt(q_p, k_p, v_p)
    ob_p = _moba_prompt(qm_p, kmb_p, vmb_p, means_p.reshape(-1, MOBA_WIDTH), slopes)

    ckv_pool = cache_mla_ckv.reshape(n_pool, PAGE_SIZE, MLA_KV_LORA)
    krt_pool = jnp.swapaxes(cache_mla_krope.reshape(n_pool, PAGE_SIZE, MLA_ROPE), 1, 2)
    kt_pool = jnp.transpose(cache_moba_k.reshape(n_pool, PAGE_SIZE, MOBA_HEADS, MOBA_HEAD_DIM),
                            (0, 2, 3, 1)).reshape(n_pool, MOBA_WIDTH, PAGE_SIZE)
    vt_pool = jnp.transpose(cache_moba_v.reshape(n_pool, PAGE_SIZE, MOBA_HEADS, MOBA_HEAD_DIM),
                            (0, 2, 3, 1)).reshape(n_pool, MOBA_WIDTH, PAGE_SIZE)
    oa_s = _mla_sample(q_s, wts["wk_abs"], wts["e_r"], wts["w_uv2"], ckv_s, kr_s, ckv_pool, krt_pool, page_table)
    sel = _moba_select(qm_s, kt_pool, page_table)
    ob_s = _moba_sample(qm_s, km_s, vm_s, kt_pool, vt_pool, page_table, sel, slopes)

    x1_p, h2_p, te_p, tw_p = _post(xp, oa_p, ob_p, sg_p, mod_p, wts)
    x1_s, h2_s, te_s, tw_s = _post(xs, oa_s.astype(BF16), ob_s.astype(BF16), sg_s, mod_s, wts)

    h_all = jnp.concatenate([h2_p, h2_s], axis=0)
    x1_all = jnp.concatenate([x1_p, x1_s], axis=0)
    te_all = jnp.concatenate([te_p, te_s], axis=0)
    tw_all = jnp.concatenate([tw_p, tw_s], axis=0)
    gt2_all = jnp.concatenate([jnp.broadcast_to(mod_p[:, 5 * d:], (sp, d)), mod_s[:, 5 * d:]], axis=0)
    row, token_of_row, blk_expert = _moe_plan(te_all, MOE_ROWS)
    w_g_e, w_u_e = _deinterleave(w_gate_up[0])
    ewts = {"w_g": w_g_e, "w_u": w_u_e, "b_g": wts["b_g_e"], "b_u": wts["b_u_e"],
            "w_d": wts["w_d_e"], "b_d": wts["b_d_e"]}
    y_rows = _moe(h_all, blk_expert, token_of_row, ewts)
    y_all = _combine(y_rows, row, tw_all, x1_all, gt2_all, g_post_ffn[0].reshape(1, d))

    hm, hd = MOBA_HEADS, MOBA_HEAD_DIM
    return (y_all[:sp].reshape(bp, sp, d), y_all[sp:].reshape(bs, 1, d),
            ckv_p.reshape(1, bp, sp, MLA_KV_LORA), kr_p.reshape(1, bp, sp, MLA_ROPE),
            km_p.reshape(1, bp, sp, hm, hd), vm_p.reshape(1, bp, sp, hm, hd),
            ckv_s.reshape(1, bs, 1, MLA_KV_LORA), kr_s.reshape(1, bs, 1, MLA_ROPE),
            km_s.reshape(1, bs, 1, hm, hd), vm_s.reshape(1, bs, 1, hm, hd))
```

```python
import functools

import jax
import jax.numpy as jnp
from jax import lax
from jax.experimental import pallas as pl
from jax.experimental.pallas import tpu as pltpu

F32 = jnp.float32
BF16 = jnp.bfloat16
I32 = jnp.int32

NORM_EPS = 1e-6
NEG = -1e30
LOG2_E = 1.4426950408889634
ROPE_THETA = 10000.0

MLA_HEADS = 8
MLA_NOPE = 64
MLA_ROPE = 32
MLA_V = 64
MLA_QK = MLA_NOPE + MLA_ROPE
MLA_Q_LORA = 384
MLA_KV_LORA = 256
HEAD_LANES = 128
MOBA_HEADS = 8
MOBA_HEAD_DIM = 64
MOBA_BLOCK = 256
MOBA_TOPK = 3
MOBA_WIDTH = MOBA_HEADS * MOBA_HEAD_DIM
PAGE_SIZE = 128
PAGES_PER_BLOCK = MOBA_BLOCK // PAGE_SIZE
N_EXPERTS = 32
TOP_K = 4
SWIGLU_LIMIT = 7.0
SWIGLU_ALPHA = 1.702

ROW_TILE = 256
MLA_TQ = 1024
MLA_TK = 2048
MOBA_TQ = 1024
MOBA_TK = 2048
DECODE_PAGES = 8
SELECT_PAGES = 16
DECODE_SEQS = 4
MOE_ROWS = 256
MOE_CHUNKS = 2
COMBINE_ROWS = 128
VMEM_LIMIT = 48 * 1024 * 1024

_NT = (((1,), (1,)), ((), ()))


def _rms(x, g):
    return x * lax.rsqrt(jnp.mean(x * x, axis=-1, keepdims=True) + NORM_EPS) * g


def _dot(a, b):
    return jnp.dot(a, b, preferred_element_type=F32)


def _dot_nt(a, b):
    return lax.dot_general(a, b, _NT, preferred_element_type=F32)


def _rows_to_tiles(x, ref):
    for j in range(ref.shape[-2]):
        ref[:, j, :] = x[:, j * HEAD_LANES:(j + 1) * HEAD_LANES]


def _tiles_to_rows(ref_view):
    return jnp.concatenate([ref_view[:, j, :] for j in range(ref_view.shape[-2])], axis=1)


def _params(*sem):
    return pltpu.CompilerParams(dimension_semantics=sem, vmem_limit_bytes=VMEM_LIMIT)


def _ada_kernel(c_ref, w_ref, b_ref, o_ref):
    c = c_ref[...]
    s = (c * jax.nn.sigmoid(c)).astype(BF16)
    o_ref[...] = _dot(s, w_ref[...].astype(BF16)) + b_ref[...]


def _ada(c_all, w_ada, b_ada):
    rows, d = c_all.shape
    n = w_ada.shape[1]
    tn = 1024
    return pl.pallas_call(
        _ada_kernel,
        out_shape=jax.ShapeDtypeStruct((rows, n), F32),
        grid=(n // tn,),
        in_specs=[pl.BlockSpec((rows, d), lambda j: (0, 0)),
                  pl.BlockSpec((d, tn), lambda j: (0, j)),
                  pl.BlockSpec((1, tn), lambda j: (0, j))],
        out_specs=pl.BlockSpec((rows, tn), lambda j: (0, j)),
        compiler_params=_params("arbitrary"),
        name="ada",
    )(c_all, w_ada, b_ada.reshape(1, n))


def _inproj_kernel(x_ref, sh_ref, sc_ref, gpre_ref, wa_ref, wkr_ref, wm_ref, wg_ref, gcq_ref, gckv_ref,
                   wqm_ref, wqs_ref, tc_ref, ts_ref, wuk_ref, ekr_ref, wuv_ref, c32_ref, s32_ref,
                   q_ref, k_ref, v_ref, ckv_ref, kr_ref, qm_ref, km_ref, vm_ref, kmb_ref, vmb_ref,
                   sg_ref, mean_ref):
    x = x_ref[...]
    h = (_rms(x, gpre_ref[...]) * (1.0 + sc_ref[...]) + sh_ref[...]).astype(BF16)

    ya = _dot(h, wa_ref[...])
    cq = _rms(ya[:, :MLA_Q_LORA], gcq_ref[...]).astype(BF16)
    ckv = _rms(ya[:, MLA_Q_LORA:], gckv_ref[...])
    ckv_ref[...] = ckv
    ckv_b = ckv.astype(BF16)

    qmain = _dot(cq, wqm_ref[...])
    qswap = _dot(cq, wqs_ref[...])
    tc = tc_ref[...]
    ts = ts_ref[...]
    for hd in range(MLA_HEADS):
        sl = slice(hd * HEAD_LANES, (hd + 1) * HEAD_LANES)
        q_ref[:, sl] = (qmain[:, sl] * tc + qswap[:, sl] * ts).astype(BF16)

    ykr = _dot(h, wkr_ref[...])
    kr = ykr[:, :MLA_ROPE] * c32_ref[...] + ykr[:, MLA_ROPE:] * s32_ref[...]
    kr_ref[...] = kr
    k_ref[...] = (_dot(ckv_b, wuk_ref[...]) + _dot(kr.astype(BF16), ekr_ref[...])).astype(BF16)
    v_ref[...] = _dot(ckv_b, wuv_ref[...]).astype(BF16)

    ym = _dot(h, wm_ref[...])
    w = MOBA_WIDTH
    qm_ref[...] = (ym[:, :w] * (MOBA_HEAD_DIM ** -0.5 * LOG2_E)).astype(BF16)
    km = ym[:, w:2 * w]
    vm = ym[:, 2 * w:]
    km_ref[...] = km
    vm_ref[...] = vm
    kmb_ref[...] = km.astype(BF16)
    vmb_ref[...] = vm.astype(BF16)
    nb = mean_ref.shape[1]
    rows = km.shape[0] // nb
    for b in range(nb):
        mean_ref[0, b:b + 1, :] = jnp.mean(km[b * rows:(b + 1) * rows], axis=0, keepdims=True)

    sg_ref[...] = jax.nn.sigmoid(_dot(h, wg_ref[...])).astype(BF16)


def _inproj(x, mod, gpre, wts, tabs):
    rows, d = x.shape
    tm = min(ROW_TILE, rows)
    mrows = mod.shape[0]
    mt = tm if mrows > 1 else 1
    mrow = (lambda i: i) if mrows > 1 else (lambda i: 0)
    nb = max(tm // MOBA_BLOCK, 1)
    nt = rows // tm

    def full(a):
        return pl.BlockSpec(a.shape, lambda i: (0,) * a.ndim)

    def rowblk(n):
        return pl.BlockSpec((tm, n), lambda i: (i, 0))

    tc, ts, c32, s32 = tabs
    in_specs = [rowblk(d),
                pl.BlockSpec((mt, d), lambda i: (mrow(i), 0)),
                pl.BlockSpec((mt, d), lambda i: (mrow(i), 1)),
                full(gpre), full(wts["w_a"]), full(wts["w_kr"]), full(wts["w_m"]), full(wts["w_g"]),
                full(wts["g_cq"]), full(wts["g_ckv"]), full(wts["wq_main"]), full(wts["wq_swap"]),
                rowblk(HEAD_LANES), rowblk(HEAD_LANES), full(wts["w_uk_pad"]), full(wts["e_kr"]),
                full(wts["w_uv2"]), rowblk(MLA_ROPE), rowblk(MLA_ROPE)]
    hp = MLA_HEADS * HEAD_LANES
    out_shape = [jax.ShapeDtypeStruct((rows, hp), BF16),
                 jax.ShapeDtypeStruct((rows, hp), BF16),
                 jax.ShapeDtypeStruct((rows, MLA_HEADS * MLA_V), BF16),
                 jax.ShapeDtypeStruct((rows, MLA_KV_LORA), F32),
                 jax.ShapeDtypeStruct((rows, MLA_ROPE), F32),
                 jax.ShapeDtypeStruct((rows, MOBA_WIDTH), BF16),
                 jax.ShapeDtypeStruct((rows, MOBA_WIDTH), F32),
                 jax.ShapeDtypeStruct((rows, MOBA_WIDTH), F32),
                 jax.ShapeDtypeStruct((rows, MOBA_WIDTH), BF16),
                 jax.ShapeDtypeStruct((rows, MOBA_WIDTH), BF16),
                 jax.ShapeDtypeStruct((rows, 2 * d), BF16),
                 jax.ShapeDtypeStruct((nt, nb, MOBA_WIDTH), F32)]
    out_specs = [rowblk(hp), rowblk(hp), rowblk(MLA_HEADS * MLA_V), rowblk(MLA_KV_LORA), rowblk(MLA_ROPE),
                 rowblk(MOBA_WIDTH), rowblk(MOBA_WIDTH), rowblk(MOBA_WIDTH), rowblk(MOBA_WIDTH),
                 rowblk(MOBA_WIDTH), rowblk(2 * d),
                 pl.BlockSpec((1, nb, MOBA_WIDTH), lambda i: (i, 0, 0))]
    return pl.pallas_call(
        _inproj_kernel,
        out_shape=out_shape,
        grid=(nt,),
        in_specs=in_specs,
        out_specs=out_specs,
        compiler_params=_params("arbitrary"),
        name="inproj",
    )(x, mod, mod, gpre, wts["w_a"], wts["w_kr"], wts["w_m"], wts["w_g"], wts["g_cq"], wts["g_ckv"],
      wts["wq_main"], wts["wq_swap"], tc, ts, wts["w_uk_pad"], wts["e_kr"], wts["w_uv2"], c32, s32)


def _flash_update(s, v, m_ref, acc_ref, hh):
    m = m_ref[hh]
    m_new = jnp.maximum(m, jnp.max(s, axis=-1, keepdims=True))
    acc_ref[hh] = acc_ref[hh] * jnp.exp2(m - m_new) + _dot(jnp.exp2(s - m_new).astype(BF16), v)
    m_ref[hh] = m_new


def _flash_finish(acc_ref, half):
    lane = lax.broadcasted_iota(I32, acc_ref.shape[1:], 1)
    a0 = acc_ref[0]
    a1 = acc_ref[1]
    l0 = a0[:, half:half + 1]
    l1 = a1[:, 0:1]
    return jnp.where(lane < half, a0 / l0, a1 / l1)


def _with_ones(v, hh, half):
    lane = lax.broadcasted_iota(I32, v.shape, 1)
    mine = (lane < half) if hh == 0 else (lane >= half)
    return jnp.where(mine, v, jnp.ones_like(v))


def _flash_init(m_ref, acc_ref):
    m_ref[...] = jnp.full(m_ref.shape, NEG, F32)
    acc_ref[...] = jnp.zeros(acc_ref.shape, F32)


def _mla_prompt_kernel(q_ref, k_ref, v_ref, o_ref, m_ref, acc_ref, *, tq, tk):
    qi = pl.program_id(1)
    _flash_init(m_ref, acc_ref)

    def step(k0, width, masked):
        v = v_ref[pl.ds(k0, width), :]
        for hh in range(2):
            hs = slice(hh * HEAD_LANES, (hh + 1) * HEAD_LANES)
            s = _dot_nt(q_ref[:, hs], k_ref[pl.ds(k0, width), hs])
            if masked:
                rel = lax.broadcasted_iota(I32, (tq, width), 1) - lax.broadcasted_iota(I32, (tq, width), 0)
                s = jnp.where(rel <= 0, s, NEG)
            _flash_update(s, _with_ones(v, hh, MLA_V), m_ref, acc_ref, hh)

    def wide_step(kb, carry):
        step(pl.multiple_of(kb * tk, tk), tk, False)
        return carry

    q0 = qi * tq
    nwide = q0 // tk
    lax.fori_loop(0, nwide, wide_step, 0)

    def narrow_step(j, carry):
        step(pl.multiple_of(nwide * tk + j * tq, tq), tq, False)
        return carry

    lax.fori_loop(0, (q0 - nwide * tk) // tq, narrow_step, 0)
    step(pl.multiple_of(q0, tq), tq, True)
    o_ref[...] = _flash_finish(acc_ref, MLA_V).astype(BF16)


def _mla_prompt(q, k, v):
    s = q.shape[0]
    tq = min(MLA_TQ, s)
    tk = min(MLA_TK, s)
    assert tk % tq == 0 and s % tk == 0
    pairs = MLA_HEADS // 2
    return pl.pallas_call(
        functools.partial(_mla_prompt_kernel, tq=tq, tk=tk),
        out_shape=jax.ShapeDtypeStruct((s, MLA_HEADS * MLA_V), BF16),
        grid=(pairs, s // tq),
        in_specs=[pl.BlockSpec((tq, 2 * HEAD_LANES), lambda p, i: (i, p)),
                  pl.BlockSpec((s, 2 * HEAD_LANES), lambda p, i: (0, p), pipeline_mode=pl.Buffered(1)),
                  pl.BlockSpec((s, 2 * MLA_V), lambda p, i: (0, p), pipeline_mode=pl.Buffered(1))],
        out_specs=pl.BlockSpec((tq, 2 * MLA_V), lambda p, i: (i, p)),
        scratch_shapes=[pltpu.VMEM((2, tq, 1), F32), pltpu.VMEM((2, tq, HEAD_LANES), F32)],
        compiler_params=_params("arbitrary", "arbitrary"),
        name="mla_prompt",
    )(q, k, v)


def _top3_penalty(gate, valid_lane, own_lane, n_elig):
    work = jnp.where(valid_lane, gate, -jnp.inf)
    lanes = lax.broadcasted_iota(I32, gate.shape, 1)
    pen = jnp.full(gate.shape, NEG, F32)
    for j in range(MOBA_TOPK):
        mx = jnp.max(work, axis=-1, keepdims=True)
        first = jnp.min(jnp.where(work == mx, lanes, 2 * gate.shape[1]), axis=-1, keepdims=True)
        pick = lanes == first
        pen = jnp.where(pick, jnp.where(j < n_elig, 0.0, NEG), pen)
        work = jnp.where(pick, -jnp.inf, work)
    return jnp.where(own_lane, 0.0, pen)


def _moba_prompt_kernel(slope_ref, q_ref, k_ref, v_ref, mean_ref, o_ref, qa_ref, m_ref, acc_ref, *, tq, tk):
    pr = pl.program_id(0)
    qi = pl.program_id(1)
    bs = MOBA_BLOCK
    hd = MOBA_HEAD_DIM
    sub = tq // bs
    lane = lax.broadcasted_iota(I32, (tq, 2 * hd), 1)
    blk = lane & (hd - 1)
    own = qi * sub + lax.broadcasted_iota(I32, (tq, 2 * hd), 0) // bs
    q2 = q_ref[...]
    means = mean_ref[...].astype(BF16)
    means2 = jnp.concatenate([means, means], axis=0)
    for hh in range(2):
        mine = (lane < hd) if hh == 0 else (lane >= hd)
        qh = jnp.where(mine, q2, jnp.zeros_like(q2))
        gate = _dot_nt(qh, means2)
        gate = jnp.where(blk < own, gate, NEG)
        pen = _top3_penalty(gate, jnp.logical_not(mine), blk == own, own)
        qa_ref[hh] = jnp.where(mine, q2, pen.astype(BF16))
    _flash_init(m_ref, acc_ref)
    q0 = qi * tq

    def step(k0, width, diagonal):
        k = k_ref[pl.ds(k0, width), :]
        v = v_ref[pl.ds(k0, width), :]
        klane = lax.broadcasted_iota(I32, (width, 2 * hd), 1)
        kblock = (k0 // bs) + lax.broadcasted_iota(I32, (width, 2 * hd), 0) // bs
        onehot = jnp.where((klane & (hd - 1)) == kblock, 1.0, 0.0).astype(BF16)
        col = lax.broadcasted_iota(I32, (1, width), 1)
        for hh in range(2):
            kmine = (klane < hd) if hh == 0 else (klane >= hd)
            s = _dot_nt(qa_ref[hh], jnp.where(kmine, k, onehot))
            s = s + slope_ref[2 * pr + hh] * (k0 - q0 + col).astype(F32)
            if diagonal:
                r = lax.broadcasted_iota(I32, (tq, width), 0)
                c = lax.broadcasted_iota(I32, (tq, width), 1)
                s = jnp.where(c // bs == r // bs, jnp.where(c > r, NEG, s), s)
            _flash_update(s, _with_ones(v, hh, hd), m_ref, acc_ref, hh)

    step(pl.multiple_of(q0, tq), tq, True)

    def wide_step(kb, carry):
        step(pl.multiple_of(kb * tk, tk), tk, False)
        return carry

    nwide = q0 // tk
    lax.fori_loop(0, nwide, wide_step, 0)

    def narrow_step(j, carry):
        step(pl.multiple_of(nwide * tk + j * tq, tq), tq, False)
        return carry

    lax.fori_loop(0, (q0 - nwide * tk) // tq, narrow_step, 0)
    o_ref[...] = _flash_finish(acc_ref, hd).astype(BF16)


def _moba_prompt(q, k, v, means, slopes):
    s = q.shape[0]
    tq = min(MOBA_TQ, s)
    tk = min(MOBA_TK, s)
    assert tq % MOBA_BLOCK == 0 and tk % tq == 0 and s % tk == 0
    assert means.shape[0] <= MOBA_HEAD_DIM
    means = jnp.pad(means, ((0, MOBA_HEAD_DIM - means.shape[0]), (0, 0)))
    nblk = MOBA_HEAD_DIM
    pairs = MOBA_HEADS // 2
    w = 2 * MOBA_HEAD_DIM
    return pl.pallas_call(
        functools.partial(_moba_prompt_kernel, tq=tq, tk=tk),
        out_shape=jax.ShapeDtypeStruct((s, MOBA_WIDTH), BF16),
        grid=(pairs, s // tq),
        in_specs=[pl.BlockSpec(memory_space=pltpu.SMEM),
                  pl.BlockSpec((tq, w), lambda p, i: (i, p)),
                  pl.BlockSpec((s, w), lambda p, i: (0, p), pipeline_mode=pl.Buffered(1)),
                  pl.BlockSpec((s, w), lambda p, i: (0, p), pipeline_mode=pl.Buffered(1)),
                  pl.BlockSpec((nblk, w), lambda p, i: (0, p))],
        out_specs=pl.BlockSpec((tq, w), lambda p, i: (i, p)),
        scratch_shapes=[pltpu.VMEM((2, tq, w), BF16), pltpu.VMEM((2, tq, 1), F32), pltpu.VMEM((2, tq, w), F32)],
        compiler_params=_params("arbitrary", "arbitrary"),
        name="moba_prompt",
    )(slopes, q, k, v, means)


def _mla_sample_kernel(pt_ref, q_ref, wk_ref, er_ref, wuv_ref, cnew_ref, knew_ref, ckv_hbm, krt_hbm, o_ref,
                       cbuf, kbuf, sem, ql_ref, qr_ref, m_ref, l_ref, acc_ref, *, pages, nseq):
    i = pl.program_id(0)
    c = pl.program_id(1)
    nc = pl.num_programs(1)
    step = i * nc + c
    n_steps = pl.num_programs(0) * nc
    slot = step % 2
    nh = MLA_HEADS

    def page_copies(s, idx, page):
        return (pltpu.make_async_copy(ckv_hbm.at[page], cbuf.at[s, idx], sem.at[0, s]),
                pltpu.make_async_copy(krt_hbm.at[page], kbuf.at[s, idx], sem.at[1, s]))

    def fetch(ii, cc, s):
        for u in range(nseq):
            for g in range(pages):
                for cp in page_copies(s, u * pages + g, pt_ref[ii * nseq + u, cc * pages + g]):
                    cp.start(priority=g % 2)

    @pl.when(step == 0)
    def _():
        fetch(0, 0, 0)

    @pl.when(step + 1 < n_steps)
    def _():
        wrap = c + 1 == nc
        fetch(jnp.where(wrap, i + 1, i), jnp.where(wrap, 0, c + 1), 1 - slot)

    for idx in range(nseq * pages):
        for cp in page_copies(slot, idx, 0):
            cp.wait()

    @pl.when(c == 0)
    def _():
        lane = lax.broadcasted_iota(I32, (nh, nh * HEAD_LANES), 1)
        row = lax.broadcasted_iota(I32, (nh, nh * HEAD_LANES), 0)
        for u in range(nseq):
            q = q_ref[u]
            qb = jnp.where((lane // HEAD_LANES) == row, q.astype(F32), 0.0).astype(BF16)
            ql_ref[u] = _dot(qb, wk_ref[...]).astype(BF16)
            qr_ref[u] = _dot(qb, er_ref[...]).astype(BF16)
        m_ref[...] = jnp.full(m_ref.shape, NEG, F32)
        l_ref[...] = jnp.zeros(l_ref.shape, F32)
        acc_ref[...] = jnp.zeros(acc_ref.shape, F32)

    cks = [jnp.concatenate([cbuf[slot, u * pages + g].astype(BF16) for g in range(pages)], axis=0)
           for u in range(nseq)]
    ss = []
    for u in range(nseq):
        krt = jnp.concatenate([kbuf[slot, u * pages + g].astype(BF16) for g in range(pages)], axis=1)
        ss.append(_dot_nt(ql_ref[u], cks[u]) + _dot(qr_ref[u], krt))
    ps = []
    corrs = []
    for u in range(nseq):
        m = m_ref[u]
        m_new = jnp.maximum(m, jnp.max(ss[u], axis=-1, keepdims=True))
        corr = jnp.exp2(m - m_new)
        p = jnp.exp2(ss[u] - m_new)
        l_ref[u] = l_ref[u] * corr + jnp.sum(p, axis=-1, keepdims=True)
        m_ref[u] = m_new
        ps.append(p.astype(BF16))
        corrs.append(corr)
    for u in range(nseq):
        acc_ref[u] = acc_ref[u] * corrs[u] + _dot(ps[u], cks[u])

    @pl.when(c == pl.num_programs(1) - 1)
    def _():
        for u in range(nseq):
            ql = ql_ref[u]
            qr = qr_ref[u]
            cn = cnew_ref[u]
            cnb = cn.astype(BF16)
            sn = _dot_nt(ql, jnp.broadcast_to(cnb, (8, MLA_KV_LORA)))[:, :1] + \
                _dot_nt(qr, jnp.broadcast_to(knew_ref[u].astype(BF16), (8, MLA_ROPE)))[:, :1]
            m0 = m_ref[u]
            m1 = jnp.maximum(m0, sn)
            cr = jnp.exp2(m0 - m1)
            pn = jnp.exp2(sn - m1)
            l1 = l_ref[u] * cr + pn
            acc = acc_ref[u] * cr + pn.astype(BF16).astype(F32) * cnb.astype(F32)
            ol = (acc / l1).astype(BF16)
            full = _dot(ol, wuv_ref[...])
            lane = lax.broadcasted_iota(I32, full.shape, 1)
            row = lax.broadcasted_iota(I32, full.shape, 0)
            o_ref[u] = jnp.sum(jnp.where((lane // MLA_V) == row, full, 0.0), axis=0, keepdims=True)


def _mla_sample(q_s, wk, er, w_uv2, ckv_new, kr_new, ckv_pool, krt_pool, page_table):
    b, npg = page_table.shape
    pages = min(DECODE_PAGES, npg)
    nseq = DECODE_SEQS if b % DECODE_SEQS == 0 else 1
    hp = MLA_HEADS * HEAD_LANES

    def full(a):
        return pl.BlockSpec(a.shape, lambda i, c, pt: (0,) * a.ndim)

    def seq_spec(width):
        return pl.BlockSpec((nseq, 1, width), lambda i, c, pt: (i, 0, 0))

    n_pg = nseq * pages
    in_specs = [seq_spec(hp), full(wk), full(er), full(w_uv2), seq_spec(MLA_KV_LORA), seq_spec(MLA_ROPE),
                pl.BlockSpec(memory_space=pl.ANY), pl.BlockSpec(memory_space=pl.ANY)]
    grid_spec = pltpu.PrefetchScalarGridSpec(
        num_scalar_prefetch=1,
        grid=(b // nseq, npg // pages),
        in_specs=in_specs,
        out_specs=seq_spec(MLA_HEADS * MLA_V),
        scratch_shapes=[pltpu.VMEM((2, n_pg, PAGE_SIZE, MLA_KV_LORA), F32),
                        pltpu.VMEM((2, n_pg, MLA_ROPE, PAGE_SIZE), F32),
                        pltpu.SemaphoreType.DMA((2, 2)),
                        pltpu.VMEM((nseq, MLA_HEADS, MLA_KV_LORA), BF16), pltpu.VMEM((nseq, MLA_HEADS, MLA_ROPE), BF16),
                        pltpu.VMEM((nseq, MLA_HEADS, 1), F32), pltpu.VMEM((nseq, MLA_HEADS, 1), F32),
                        pltpu.VMEM((nseq, MLA_HEADS, MLA_KV_LORA), F32)])
    out = pl.pallas_call(
        functools.partial(_mla_sample_kernel, pages=pages, nseq=nseq),
        out_shape=jax.ShapeDtypeStruct((b, 1, MLA_HEADS * MLA_V), F32),
        grid_spec=grid_spec,
        compiler_params=_params("arbitrary", "arbitrary"),
        name="mla_sample",
    )(page_table, q_s.reshape(b, 1, hp), wk, er, w_uv2, ckv_new.reshape(b, 1, MLA_KV_LORA),
      kr_new.reshape(b, 1, MLA_ROPE), ckv_pool, krt_pool)
    return out.reshape(b, MLA_HEADS * MLA_V)


def _moba_select_kernel(pt_ref, q_ref, k_hbm, sel_ref, kbuf, sem, bsum_ref, *, pages, nblk):
    i = pl.program_id(0)
    c = pl.program_id(1)
    nc = pl.num_programs(1)
    step = i * nc + c
    n_steps = pl.num_programs(0) * nc
    slot = step % 2

    def page_copy(s, g, page):
        return pltpu.make_async_copy(k_hbm.at[page], kbuf.at[s, g], sem.at[s])

    def fetch(ii, cc, s):
        for g in range(pages):
            page_copy(s, g, pt_ref[ii, cc * pages + g]).start(priority=g % 2)

    @pl.when(step == 0)
    def _():
        fetch(0, 0, 0)

    @pl.when(step + 1 < n_steps)
    def _():
        wrap = c + 1 == nc
        fetch(jnp.where(wrap, i + 1, i), jnp.where(wrap, 0, c + 1), 1 - slot)

    for g in range(pages):
        page_copy(slot, g, 0).wait()

    @pl.when(c == 0)
    def _():
        bsum_ref[...] = jnp.zeros(bsum_ref.shape, F32)

    blocks = pages // PAGES_PER_BLOCK
    lane = lax.broadcasted_iota(I32, bsum_ref.shape, 1)
    acc = bsum_ref[...]
    for gb in range(blocks):
        x = kbuf[slot, gb * PAGES_PER_BLOCK]
        for t in range(1, PAGES_PER_BLOCK):
            x = x + kbuf[slot, gb * PAGES_PER_BLOCK + t]
        acc = jnp.where(lane == c * blocks + gb, jnp.sum(x, axis=-1, keepdims=True), acc)
    bsum_ref[...] = acc

    @pl.when(c == nc - 1)
    def _():
        means_t = (bsum_ref[...] * (1.0 / MOBA_BLOCK)).astype(BF16)
        nh = MOBA_HEADS
        q = q_ref[0]
        hl = lax.broadcasted_iota(I32, (nh, MOBA_WIDTH), 1)
        row = lax.broadcasted_iota(I32, (nh, MOBA_WIDTH), 0)
        qb = jnp.where((hl // MOBA_HEAD_DIM) == row, q.astype(F32), 0.0).astype(BF16)
        gate = _dot(qb, means_t)
        lanes = lax.broadcasted_iota(I32, gate.shape, 1)
        work = jnp.where(lanes < nblk, gate, -jnp.inf)
        out = jnp.zeros(gate.shape, I32)
        for j in range(min(MOBA_TOPK, nblk)):
            mx = jnp.max(work, axis=-1, keepdims=True)
            first = jnp.min(jnp.where(work == mx, lanes, HEAD_LANES), axis=-1, keepdims=True)
            out = jnp.where(lanes == j, first, out)
            work = jnp.where(lanes == first, -jnp.inf, work)
        sel_ref[0] = out


def _moba_select(qm_s, kt_pool, page_table):
    b, npg = page_table.shape
    pages = min(SELECT_PAGES, npg)
    nblk = npg // PAGES_PER_BLOCK
    assert pages % PAGES_PER_BLOCK == 0 and npg % pages == 0 and nblk <= HEAD_LANES
    grid_spec = pltpu.PrefetchScalarGridSpec(
        num_scalar_prefetch=1,
        grid=(b, npg // pages),
        in_specs=[pl.BlockSpec((1, 1, MOBA_WIDTH), lambda i, c, pt: (i, 0, 0)), pl.BlockSpec(memory_space=pl.ANY)],
        out_specs=pl.BlockSpec((1, MOBA_HEADS, HEAD_LANES), lambda i, c, pt: (i, 0, 0)),
        scratch_shapes=[pltpu.VMEM((2, pages, MOBA_WIDTH, PAGE_SIZE), F32), pltpu.SemaphoreType.DMA((2,)),
                        pltpu.VMEM((MOBA_WIDTH, HEAD_LANES), F32)])
    sel = pl.pallas_call(
        functools.partial(_moba_select_kernel, pages=pages, nblk=nblk),
        out_shape=jax.ShapeDtypeStruct((b, MOBA_HEADS, HEAD_LANES), I32),
        grid_spec=grid_spec,
        compiler_params=_params("arbitrary", "arbitrary"),
        name="moba_select",
    )(page_table, qm_s.reshape(b, 1, MOBA_WIDTH), kt_pool)
    return sel[:, :, :MOBA_TOPK]


def _moba_sample_kernel(pg_ref, sel_ref, slope_ref, q_ref, kn_ref, vn_ref, k_hbm, v_hbm, o_ref, kbuf, vbuf, sem,
                        *, n_sel, past):
    nh = MOBA_HEADS
    hd = MOBA_HEAD_DIM
    per_head = n_sel * PAGES_PER_BLOCK
    n_pg = nh * per_head
    i = pl.program_id(0)
    n_steps = pl.num_programs(0)
    slot = i % 2

    def slab_copies(s, idx, page):
        rows = pl.ds((idx // per_head) * hd, hd)
        return (pltpu.make_async_copy(k_hbm.at[page, rows, :], kbuf.at[s, idx], sem.at[0, s]),
                pltpu.make_async_copy(v_hbm.at[page, rows, :], vbuf.at[s, idx], sem.at[1, s]))

    def fetch(stp, s):
        for idx in range(n_pg):
            for cp in slab_copies(s, idx, pg_ref[stp * n_pg + idx]):
                cp.start()

    @pl.when(i == 0)
    def _():
        fetch(0, 0)

    @pl.when(i + 1 < n_steps)
    def _():
        fetch(i + 1, 1 - slot)

    for idx in range(n_pg):
        for cp in slab_copies(slot, idx, 0):
            cp.wait()

    q = q_ref[0]
    kn = kn_ref[0].astype(BF16)
    vn = vn_ref[0].astype(BF16).astype(F32)
    col = lax.broadcasted_iota(I32, (1, PAGE_SIZE), 1)
    for h in range(nh):
        q8 = jnp.broadcast_to(q[h:h + 1, :], (8, hd))
        slope = slope_ref[h]
        sn = _dot_nt(q8, jnp.broadcast_to(kn[h:h + 1, :], (8, hd)))[:1, :1]
        ss = []
        for j in range(n_sel):
            blk = sel_ref[(i * nh + h) * n_sel + j]
            for t in range(PAGES_PER_BLOCK):
                kt = kbuf[slot, h * per_head + j * PAGES_PER_BLOCK + t].astype(BF16)
                dist = (past - blk * MOBA_BLOCK - t * PAGE_SIZE - col).astype(F32)
                ss.append(_dot(q8, kt)[:1, :] - slope * dist)
        m = sn
        for s in ss:
            m = jnp.maximum(m, jnp.max(s, axis=-1, keepdims=True))
        pn = jnp.exp2(sn - m)
        l = pn
        acc = pn.astype(BF16).astype(F32) * vn[h:h + 1, :]
        for idx, s in enumerate(ss):
            p = jnp.exp2(s - m)
            l = l + jnp.sum(p, axis=-1, keepdims=True)
            vt = vbuf[slot, h * per_head + idx].astype(BF16)
            acc = acc + _dot_nt(jnp.broadcast_to(p.astype(BF16), (8, PAGE_SIZE)), vt)[:1, :]
        o_ref[0, h:h + 1, :] = acc / l


def _moba_sample(qm_s, km_s, vm_s, kt_pool, vt_pool, page_table, sel, slopes):
    b, npg = page_table.shape
    n_sel = sel.shape[-1]
    nh, hd = MOBA_HEADS, MOBA_HEAD_DIM
    past = npg * PAGE_SIZE
    ppb = PAGES_PER_BLOCK
    n_pg = nh * n_sel * ppb
    logical = sel[..., None] * ppb + jnp.arange(ppb, dtype=I32)
    phys = jnp.take_along_axis(page_table, logical.reshape(b, -1), axis=1).reshape(-1)

    def tok_spec():
        return pl.BlockSpec((1, nh, hd), lambda i, pg, sl: (i, 0, 0))

    grid_spec = pltpu.PrefetchScalarGridSpec(
        num_scalar_prefetch=2,
        grid=(b,),
        in_specs=[pl.BlockSpec(memory_space=pltpu.SMEM), tok_spec(), tok_spec(), tok_spec(),
                  pl.BlockSpec(memory_space=pl.ANY), pl.BlockSpec(memory_space=pl.ANY)],
        out_specs=tok_spec(),
        scratch_shapes=[pltpu.VMEM((2, n_pg, hd, PAGE_SIZE), F32), pltpu.VMEM((2, n_pg, hd, PAGE_SIZE), F32),
                        pltpu.SemaphoreType.DMA((2, 2))])
    out = pl.pallas_call(
        functools.partial(_moba_sample_kernel, n_sel=n_sel, past=past),
        out_shape=jax.ShapeDtypeStruct((b, nh, hd), F32),
        grid_spec=grid_spec,
        compiler_params=_params("arbitrary"),
        name="moba_sample",
    )(phys, sel.reshape(-1), slopes, qm_s.reshape(b, nh, hd), km_s.reshape(b, nh, hd), vm_s.reshape(b, nh, hd),
      kt_pool, vt_pool)
    return out.reshape(b, MOBA_WIDTH)


def _post_kernel(x_ref, oa_ref, ob_ref, sga_ref, sgb_ref, gt1_ref, sh2_ref, sc2_ref, wba_ref, wbb_ref, wo_ref,
                 gpm_ref, gpf_ref, wr_ref, br_ref, x1_ref, h2_ref, te_ref, tw_ref):
    merged = (sga_ref[...].astype(F32) * _dot(oa_ref[...], wba_ref[...])
              + sgb_ref[...].astype(F32) * _dot(ob_ref[...], wbb_ref[...]))
    mix = _dot(merged.astype(BF16), wo_ref[...])
    x1 = x_ref[...] + gt1_ref[...] * _rms(mix, gpm_ref[...])
    x1_ref[...] = x1
    h2 = _rms(x1, gpf_ref[...]) * (1.0 + sc2_ref[...]) + sh2_ref[...]
    _rows_to_tiles(h2, h2_ref)
    logits = _dot(h2.astype(BF16), wr_ref[...]) + br_ref[...]
    ne = logits.shape[1]
    lanes = lax.broadcasted_iota(I32, logits.shape, 1)
    out_lane = lax.broadcasted_iota(I32, (logits.shape[0], TOP_K), 1)
    te = jnp.zeros((logits.shape[0], TOP_K), I32)
    tl = jnp.zeros((logits.shape[0], TOP_K), F32)
    work = logits
    for j in range(TOP_K):
        mx = jnp.max(work, axis=-1, keepdims=True)
        first = jnp.min(jnp.where(work == mx, lanes, ne), axis=-1, keepdims=True)
        te = jnp.where(out_lane == j, first, te)
        tl = jnp.where(out_lane == j, mx, tl)
        work = jnp.where(lanes == first, -jnp.inf, work)
    e = jnp.exp(tl - jnp.max(tl, axis=-1, keepdims=True))
    te_ref[...] = te
    tw_ref[...] = e / jnp.sum(e, axis=-1, keepdims=True)


def _post(x, o_a, o_b, sg, mod, wts):
    rows, d = x.shape
    tm = min(ROW_TILE, rows)
    mrows = mod.shape[0]
    mt = tm if mrows > 1 else 1
    mrow = (lambda i: i) if mrows > 1 else (lambda i: 0)

    def full(a):
        return pl.BlockSpec(a.shape, lambda i: (0,) * a.ndim)

    def rowblk(n, j=0):
        return pl.BlockSpec((tm, n), lambda i: (i, j))

    def modblk(j):
        return pl.BlockSpec((mt, d), lambda i: (mrow(i), j))

    names = ["w_br_mla", "w_br_moba", "w_out", "g_post_mix", "g_pre_ffn", "w_router", "b_router"]
    in_specs = ([rowblk(d), rowblk(MLA_HEADS * MLA_V), rowblk(MOBA_WIDTH), rowblk(d, 0), rowblk(d, 1),
                 modblk(2), modblk(3), modblk(4)] + [full(wts[n]) for n in names])
    return pl.pallas_call(
        _post_kernel,
        out_shape=[jax.ShapeDtypeStruct((rows, d), F32), jax.ShapeDtypeStruct((rows, d // HEAD_LANES, HEAD_LANES), F32),
                   jax.ShapeDtypeStruct((rows, TOP_K), I32), jax.ShapeDtypeStruct((rows, TOP_K), F32)],
        grid=(rows // tm,),
        in_specs=in_specs,
        out_specs=[rowblk(d), pl.BlockSpec((tm, d // HEAD_LANES, HEAD_LANES), lambda i: (i, 0, 0)),
                   rowblk(TOP_K), rowblk(TOP_K)],
        compiler_params=_params("arbitrary"),
        name="post",
    )(x, o_a, o_b, sg, sg, mod, mod, mod, *[wts[n] for n in names])


def _deinterleave_kernel(w_ref, p_ref, g_ref, u_ref):
    half = g_ref.shape[2]
    y = _dot(w_ref[0].astype(BF16), p_ref[...])
    g_ref[0] = y[:, :half].astype(BF16)
    u_ref[0] = y[:, half:].astype(BF16)


def _deinterleave(w_gate_up):
    ne, d, two_f = w_gate_up.shape
    chunk = 512
    half = chunk // 2
    idx = jnp.arange(chunk, dtype=I32)
    dest = jnp.where(idx % 2 == 0, idx // 2, half + idx // 2)
    perm = (dest[:, None] == jnp.arange(chunk, dtype=I32)[None, :]).astype(BF16)
    out = jax.ShapeDtypeStruct((ne, d, two_f // 2), BF16)
    return pl.pallas_call(
        _deinterleave_kernel,
        out_shape=[out, out],
        grid=(ne, two_f // chunk),
        in_specs=[pl.BlockSpec((1, d, chunk), lambda e, c: (e, 0, c)),
                  pl.BlockSpec((chunk, chunk), lambda e, c: (0, 0))],
        out_specs=[pl.BlockSpec((1, d, half), lambda e, c: (e, 0, c)),
                   pl.BlockSpec((1, d, half), lambda e, c: (e, 0, c))],
        compiler_params=_params("arbitrary", "arbitrary"),
        name="deinterleave",
    )(w_gate_up, perm)


def _moe_kernel(be_ref, tor_ref, h_hbm, wg_ref, wu_ref, bg_ref, bu_ref, wd_ref, bd_ref, y_ref, xbuf, sem, acc_ref,
                *, rb, chunks):
    i = pl.program_id(0)
    n = pl.num_programs(0)
    slot = i % 2
    dff = wg_ref.shape[2]
    fc = dff // chunks
    rc = rb // chunks

    def gather_start(blk, s, r0, rows):
        for r in range(rows):
            tok = tor_ref[blk * rb + r0 + r]
            pltpu.make_async_copy(h_hbm.at[tok], xbuf.at[s, r0 + r], sem.at[s]).start(priority=r % 2)

    def gather_wait(s):
        pltpu.make_async_copy(h_hbm.at[pl.ds(0, rb)], xbuf.at[s], sem.at[s]).wait()

    @pl.when(i == 0)
    def _():
        gather_start(0, 0, 0, rb)

    gather_wait(slot)
    x = _tiles_to_rows(xbuf.at[slot]).astype(BF16)
    nxt = jnp.minimum(i + 1, n - 1)

    def piece(c, carry):
        gather_start(nxt, 1 - slot, pl.multiple_of(c * rc, rc), rc)
        f0 = pl.multiple_of(c * fc, fc)
        g = jnp.minimum(_dot(x, wg_ref[0, :, pl.ds(f0, fc)]) + bg_ref[0, :, pl.ds(f0, fc)], SWIGLU_LIMIT)
        u = jnp.clip(_dot(x, wu_ref[0, :, pl.ds(f0, fc)]) + bu_ref[0, :, pl.ds(f0, fc)], -SWIGLU_LIMIT, SWIGLU_LIMIT)
        a = (u + 1.0) * (g * jax.nn.sigmoid(SWIGLU_ALPHA * g))
        part = _dot(a.astype(BF16), wd_ref[0, pl.ds(f0, fc), :])

        @pl.when(c == 0)
        def _():
            acc_ref[...] = part + bd_ref[0]

        @pl.when(c > 0)
        def _():
            acc_ref[...] = acc_ref[...] + part

        return carry

    lax.fori_loop(0, chunks, piece, 0)
    _rows_to_tiles(acc_ref[...], y_ref)

    @pl.when(i == n - 1)
    def _():
        gather_wait(1 - slot)


def _moe(h_all, blk_expert, token_of_row, wts):
    t, nt, _ = h_all.shape
    d = nt * HEAD_LANES
    rb = MOE_ROWS
    n_blk = blk_expert.shape[0]
    dff = wts["w_g"].shape[2]

    def wspec(r, c):
        return pl.BlockSpec((1, r, c), lambda i, be, tor: (be[i], 0, 0))

    grid_spec = pltpu.PrefetchScalarGridSpec(
        num_scalar_prefetch=2,
        grid=(n_blk,),
        in_specs=[pl.BlockSpec(memory_space=pl.ANY), wspec(d, dff), wspec(d, dff), wspec(1, dff), wspec(1, dff),
                  wspec(dff, d), wspec(1, d)],
        out_specs=pl.BlockSpec((rb, nt, HEAD_LANES), lambda i, be, tor: (i, 0, 0)),
        scratch_shapes=[pltpu.VMEM((2, rb, nt, HEAD_LANES), F32), pltpu.SemaphoreType.DMA((2,)),
                        pltpu.VMEM((rb, d), F32)])
    return pl.pallas_call(
        functools.partial(_moe_kernel, rb=rb, chunks=MOE_CHUNKS),
        out_shape=jax.ShapeDtypeStruct((n_blk * rb, nt, HEAD_LANES), F32),
        grid_spec=grid_spec,
        compiler_params=_params("arbitrary"),
        name="moe",
    )(blk_expert, token_of_row, h_all, wts["w_g"], wts["w_u"], wts["b_g"], wts["b_u"], wts["w_d"], wts["b_d"])


def _combine_kernel(row_ref, y_hbm, tw_ref, x1_ref, gt2_ref, gpost_ref, o_ref, buf, sem, *, tm):
    i = pl.program_id(0)
    n = pl.num_programs(0)
    slot = i % 2

    def gather_start(tile, s):
        def body(r, carry):
            for kk in range(TOP_K):
                src = row_ref[(tile * tm + r) * TOP_K + kk]
                pltpu.make_async_copy(y_hbm.at[src], buf.at[s, kk, r], sem.at[s]).start(priority=kk % 2)
            return carry
        lax.fori_loop(0, tm, body, 0)

    @pl.when(i == 0)
    def _():
        gather_start(0, 0)

    @pl.when(i + 1 < n)
    def _():
        gather_start(i + 1, 1 - slot)

    for kk in range(TOP_K):
        pltpu.make_async_copy(y_hbm.at[pl.ds(0, tm)], buf.at[slot, kk], sem.at[slot]).wait()
    tw = tw_ref[...]
    y = tw[:, 0:1] * _tiles_to_rows(buf.at[slot, 0])
    for kk in range(1, TOP_K):
        y = y + tw[:, kk:kk + 1] * _tiles_to_rows(buf.at[slot, kk])
    o_ref[...] = x1_ref[...] + gt2_ref[...] * _rms(y, gpost_ref[...])


def _combine(y_rows, row, tw, x1, gt2, gpost):
    t, d = x1.shape
    tm = min(COMBINE_ROWS, t)
    assert t % tm == 0

    def rowblk(n):
        return pl.BlockSpec((tm, n), lambda i, r: (i, 0))

    grid_spec = pltpu.PrefetchScalarGridSpec(
        num_scalar_prefetch=1,
        grid=(t // tm,),
        in_specs=[pl.BlockSpec(memory_space=pl.ANY), rowblk(TOP_K), rowblk(d), rowblk(d),
                  pl.BlockSpec((1, d), lambda i, r: (0, 0))],
        out_specs=rowblk(d),
        scratch_shapes=[pltpu.VMEM((2, TOP_K, tm, d // HEAD_LANES, HEAD_LANES), F32), pltpu.SemaphoreType.DMA((2,))])
    return pl.pallas_call(
        functools.partial(_combine_kernel, tm=tm),
        out_shape=jax.ShapeDtypeStruct((t, d), F32),
        grid_spec=grid_spec,
        compiler_params=_params("arbitrary"),
        name="combine",
    )(row, y_rows, tw, x1, gt2, gpost)


def _rope_tables(pos):
    half = MLA_ROPE // 2
    inv_freq = ROPE_THETA ** (-jnp.arange(half, dtype=F32) / half)
    ang = pos.astype(F32)[:, None] * inv_freq[None, :]
    cos, sin = jnp.cos(ang), jnp.sin(ang)
    c32 = jnp.concatenate([cos, cos], axis=-1)
    s32 = jnp.concatenate([sin, sin], axis=-1)
    n = pos.shape[0]
    scale = MLA_QK ** -0.5 * LOG2_E
    pad =jnp.zeros((n, HEAD_LANES - MLA_QK), F32)
    tc = jnp.concatenate([jnp.full((n, MLA_NOPE), scale, F32), c32 * scale, pad], axis=-1)
    ts = jnp.concatenate([jnp.zeros((n, MLA_NOPE), F32), s32 * scale, pad], axis=-1)
    return tc, ts, c32, s32


def _swap_halves(w):
    half = w.shape[-1] // 2
    return jnp.concatenate([-w[..., half:], w[..., :half]], axis=-1)


def _prepare_weights(w_in, g_cq, w_uq, g_ckv, w_uk, w_uv, w_br_mla, w_br_moba, w_out, g_post_mix, g_pre_ffn,
                     w_router, b_router, b_gate_up, w_down, b_down):
    d = w_in.shape[0]
    o_ckv = MLA_Q_LORA + MLA_KV_LORA
    o_kr = o_ckv + MLA_ROPE
    o_m = o_kr + 3 * MOBA_WIDTH
    w_kr = w_in[:, o_ckv:o_kr]
    uq = w_uq.reshape(MLA_Q_LORA, MLA_HEADS, MLA_QK)
    zq = jnp.zeros((MLA_Q_LORA, MLA_HEADS, HEAD_LANES - MLA_QK), F32)
    wq_main = jnp.concatenate([uq, zq], axis=-1).reshape(MLA_Q_LORA, -1)
    wq_swap = jnp.concatenate([jnp.zeros((MLA_Q_LORA, MLA_HEADS, MLA_NOPE), F32),
                               _swap_halves(uq[..., MLA_NOPE:]), zq], axis=-1).reshape(MLA_Q_LORA, -1)
    w_uk_pad = jnp.concatenate([w_uk, jnp.zeros((MLA_KV_LORA, MLA_HEADS, HEAD_LANES - MLA_NOPE), F32)],
                               axis=-1).reshape(MLA_KV_LORA, -1)
    eye = jnp.eye(MLA_ROPE, dtype=F32)
    e_head = jnp.concatenate([jnp.zeros((MLA_ROPE, MLA_NOPE), F32), eye,
                              jnp.zeros((MLA_ROPE, HEAD_LANES - MLA_QK), F32)], axis=-1)
    e_kr = jnp.tile(e_head, (1, MLA_HEADS))
    wk_abs = jnp.transpose(w_uk_pad.reshape(MLA_KV_LORA, -1))
    e_r = jnp.transpose(jnp.tile(e_head, (1, MLA_HEADS)))
    ne, two_f = b_gate_up.shape
    bf = lambda a: a.astype(BF16)
    return {
        "w_a": bf(w_in[:, :o_ckv]), "w_kr": bf(jnp.concatenate([w_kr, _swap_halves(w_kr)], axis=-1)),
        "w_m": bf(w_in[:, o_kr:o_m]), "w_g": bf(w_in[:, o_m:]),
        "g_cq": g_cq.reshape(1, -1), "g_ckv": g_ckv.reshape(1, -1),
        "wq_main": bf(wq_main), "wq_swap": bf(wq_swap), "w_uk_pad": bf(w_uk_pad), "e_kr": bf(e_kr),
        "w_uv2": bf(w_uv.reshape(MLA_KV_LORA, -1)), "wk_abs": bf(wk_abs), "e_r": bf(e_r),
        "w_br_mla": bf(w_br_mla), "w_br_moba": bf(w_br_moba), "w_out": bf(w_out),
        "g_post_mix": g_post_mix.reshape(1, -1), "g_pre_ffn": g_pre_ffn.reshape(1, -1),
        "w_router": bf(w_router), "b_router": b_router.reshape(1, -1),
        "b_g_e": b_gate_up[:, 0::2].reshape(ne, 1, two_f // 2), "b_u_e": b_gate_up[:, 1::2].reshape(ne, 1, two_f // 2),
        "w_d_e": bf(w_down), "b_d_e": b_down.reshape(ne, 1, d),
    }


def _moe_plan(top_e, rb):
    a = top_e.size
    flat_e = top_e.reshape(a)
    onehot = flat_e[:, None] == jnp.arange(N_EXPERTS, dtype=I32)[None, :]
    chunk = 256
    assert a % chunk == 0
    oh = onehot.astype(BF16).reshape(a // chunk, chunk, N_EXPERTS)
    tri = (jnp.arange(chunk)[:, None] > jnp.arange(chunk)[None, :]).astype(BF16)
    within = jnp.einsum("ij,cje->cie", tri, oh, preferred_element_type=F32)
    totals = jnp.sum(oh.astype(F32), axis=1)
    before = jnp.cumsum(totals, axis=0) - totals
    counts = jnp.sum(totals, axis=0).astype(I32)
    rank_all = (within + before[:, None, :]).reshape(a, N_EXPERTS)
    rank = jnp.sum(jnp.where(onehot, rank_all, 0.0), axis=1).astype(I32)
    padded = (counts + rb - 1) // rb * rb
    pad_end = jnp.cumsum(padded)
    pad_start = pad_end - padded
    row = (jnp.sum(jnp.where(onehot, pad_start[None, :], 0), axis=1) + rank).astype(I32)
    n_blk = -(-a // rb) + N_EXPERTS
    token_of_row = jnp.zeros((n_blk * rb,), I32).at[row].set(jnp.arange(a, dtype=I32) // TOP_K)
    blk_start = jnp.arange(n_blk, dtype=I32) * rb
    blk_expert = jnp.minimum(jnp.sum((blk_start[:, None] >= pad_end[None, :]).astype(I32), axis=1), N_EXPERTS - 1)
    return row, token_of_row, blk_expert


def kernel(x_prompt, x_sample, cache_mla_ckv, cache_mla_krope, cache_moba_k, cache_moba_v, page_table, c_prompt,
           c_sample, w_ada, b_ada, g_pre_mix, g_post_mix, g_pre_ffn, g_post_ffn, w_in, g_cq, w_uq, g_ckv, w_uk,
           w_uv, w_br_mla, w_br_moba, w_out, w_router, b_router, w_gate_up, b_gate_up, w_down, b_down):
    depth = w_in.shape[0]
    assert depth == 1 and x_prompt.shape[0] == 1 and x_sample.shape[1] == 1
    bp, sp, d = x_prompt.shape
    bs = x_sample.shape[0]
    n_pool = cache_mla_ckv.shape[1]
    npg = page_table.shape[1]
    assert npg % PAGES_PER_BLOCK == 0 and sp % MOBA_BLOCK == 0
    past = npg * PAGE_SIZE

    wts = _prepare_weights(w_in[0], g_cq[0], w_uq[0], g_ckv[0], w_uk[0], w_uv[0], w_br_mla[0], w_br_moba[0],
                           w_out[0], g_post_mix[0], g_pre_ffn[0], w_router[0], b_router[0], b_gate_up[0],
                           w_down[0], b_down[0])
    slopes = 2.0 ** (-8.0 * jnp.arange(1, MOBA_HEADS + 1, dtype=F32) / MOBA_HEADS) * LOG2_E

    c_all = jnp.concatenate([c_sample, c_prompt, jnp.zeros((7, d), F32)], axis=0)
    mod = _ada(c_all, w_ada[0], b_ada[0])
    mod_s, mod_p = mod[:bs], mod[bs:bs + 1]
    gpre = g_pre_mix[0].reshape(1, d)

    xp = x_prompt.reshape(sp, d)
    xs = x_sample.reshape(bs, d)
    tabs_p = _rope_tables(jnp.arange(sp))
    tabs_s = _rope_tables(jnp.full((bs,), past))

    (q_p, k_p, v_p, ckv_p, kr_p, qm_p, km_p, vm_p, kmb_p, vmb_p, sg_p, means_p) = _inproj(xp, mod_p, gpre, wts, tabs_p)
    (q_s, _, _, ckv_s, kr_s, qm_s, km_s, vm_s, _, _, sg_s, _) = _inproj(xs, mod_s, gpre, wts, tabs_s)

    oa_p = _mla_prompt(q_p, k_p, v_p)
    ob_p = _moba_prompt(qm_p, kmb_p, vmb_p, means_p.reshape(-1, MOBA_WIDTH), slopes)

    ckv_pool = cache_mla_ckv.reshape(n_pool, PAGE_SIZE, MLA_KV_LORA)
    krt_pool = jnp.swapaxes(cache_mla_krope.reshape(n_pool, PAGE_SIZE, MLA_ROPE), 1, 2)
    kt_pool = jnp.transpose(cache_moba_k.reshape(n_pool, PAGE_SIZE, MOBA_HEADS, MOBA_HEAD_DIM),
                            (0, 2, 3, 1)).reshape(n_pool, MOBA_WIDTH, PAGE_SIZE)
    vt_pool = jnp.transpose(cache_moba_v.reshape(n_pool, PAGE_SIZE, MOBA_HEADS, MOBA_HEAD_DIM),
                            (0, 2, 3, 1)).reshape(n_pool, MOBA_WIDTH, PAGE_SIZE)
    oa_s = _mla_sample(q_s, wts["wk_abs"], wts["e_r"], wts["w_uv2"], ckv_s, kr_s, ckv_pool, krt_pool, page_table)
    sel = _moba_select(qm_s, kt_pool, page_table)
    ob_s = _moba_sample(qm_s, km_s, vm_s, kt_pool, vt_pool, page_table, sel, slopes)

    x1_p, h2_p, te_p, tw_p = _post(xp, oa_p, ob_p, sg_p, mod_p, wts)
    x1_s, h2_s, te_s, tw_s = _post(xs, oa_s.astype(BF16), ob_s.astype(BF16), sg_s, mod_s, wts)

    h_all = jnp.concatenate([h2_p, h2_s], axis=0)
    x1_all = jnp.concatenate([x1_p, x1_s], axis=0)
    te_all = jnp.concatenate([te_p, te_s], axis=0)
    tw_all = jnp.concatenate([tw_p, tw_s], axis=0)
    gt2_all = jnp.concatenate([jnp.broadcast_to(mod_p[:, 5 * d:], (sp, d)), mod_s[:, 5 * d:]], axis=0)
    row, token_of_row, blk_expert = _moe_plan(te_all, MOE_ROWS)
    w_g_e, w_u_e = _deinterleave(w_gate_up[0])
    ewts = {"w_g": w_g_e, "w_u": w_u_e, "b_g": wts["b_g_e"], "b_u": wts["b_u_e"],
            "w_d": wts["w_d_e"], "b_d": wts["b_d_e"]}
    y_rows = _moe(h_all, blk_expert, token_of_row, ewts)
    y_all = _combine(y_rows, row, tw_all, x1_all, gt2_all, g_post_ffn[0].reshape(1, d))

    hm, hd = MOBA_HEADS, MOBA_HEAD_DIM
    return (y_all[:sp].reshape(bp, sp, d), y_all[sp:].reshape(bs, 1, d),
            ckv_p.reshape(1, bp, sp, MLA_KV_LORA), kr_p.reshape(1, bp, sp, MLA_ROPE),
            km_p.reshape(1, bp, sp, hm, hd), vm_p.reshape(1, bp, sp, hm, hd),
            ckv_s.reshape(1, bs, 1, MLA_KV_LORA), kr_s.reshape(1, bs, 1, MLA_ROPE),
            km_s.reshape(1, bs, 1, hm, hd), vm_s.reshape(1, bs, 1, hm, hd))
```
